```python
import math, functools
import jax, jax.numpy as jnp
from jax import lax
import numpy as np


D_MODEL = 1024
BATCH = 4
SEQ = 8192
DEPTH = 2

GRID_W = 64
CTX_LEN = 256
BRANCH_WIDTH = D_MODEL // 2
N_BRANCH = 4
NA_HEAD_DIM = 64
NA_HEADS = BRANCH_WIDTH // NA_HEAD_DIM
NA_WIDTH = NA_HEADS * NA_HEAD_DIM
NA_WIN_ROWS = 8
NA_WIN_COLS = 16
LRU_WIDTH = BRANCH_WIDTH
LRU_BLOCKS = 8
LRU_BLOCK = LRU_WIDTH // LRU_BLOCKS
LRU_C = 8.0
CONV_W = 4
GDN_HEAD_DIM = 128
GDN_HEADS = BRANCH_WIDTH // GDN_HEAD_DIM
GDN_WIDTH = GDN_HEADS * GDN_HEAD_DIM
RET_V_DIM = 128
RET_HEADS = BRANCH_WIDTH // RET_V_DIM
RET_QK_DIM = 64
RET_WIDTH = RET_HEADS * RET_V_DIM
CHUNK = 64
ROPE_BASE = 10000.0
NORM_EPS = 1e-6
L2_EPS = 1e-6
NEG_INF = -1e30

IN_LAYOUT = (
    ('na_q', NA_WIDTH), ('na_k', NA_WIDTH), ('na_v', NA_WIDTH), ('na_z', NA_WIDTH),
    ('lru_x', LRU_WIDTH), ('lru_z', LRU_WIDTH),
    ('gdn_qkv', 3 * GDN_WIDTH), ('gdn_ab', 4 * GDN_HEADS), ('gdn_z', GDN_WIDTH),
    ('ret_q', RET_HEADS * RET_QK_DIM), ('ret_k', RET_HEADS * RET_QK_DIM), ('ret_v', RET_WIDTH), ('ret_z', RET_WIDTH),
    ('merge', N_BRANCH * D_MODEL),
)
D_IN = sum(w for _, w in IN_LAYOUT)

kernel_name = 'hybrid_na_rglru_gdn_retention_block'


def split_columns(u):
    bounds = np.cumsum([w for _, w in IN_LAYOUT])[:-1].tolist()
    return dict(zip([name for name, _ in IN_LAYOUT], jnp.split(u, bounds, axis=-1)))


def heads(a, n_heads):
    return a.reshape(*a.shape[:-1], n_heads, a.shape[-1] // n_heads)


def rms(a):
    a = a.astype(jnp.float32)
    return a * lax.rsqrt(jnp.mean(a * a, axis=-1, keepdims=True) + NORM_EPS)


def rmsnorm(a, g):
    return (rms(a) * g.astype(jnp.float32)).astype(a.dtype)


def l2norm(a):
    return a * lax.rsqrt(jnp.sum(a * a, axis=-1, keepdims=True) + L2_EPS)


def dwconv(x, w):
    left = CONV_W // 2
    right = CONV_W - 1 - left
    return lax.conv_general_dilated(x, w[:, None, :].astype(x.dtype), window_strides=(1,),
                                    padding=[(left, right)], dimension_numbers=('NWC', 'WIO', 'NWC'),
                                    feature_group_count=x.shape[-1])


def axial_rope(n, dim):
    t = jnp.arange(n)
    row = (t // GRID_W).astype(jnp.float32)
    col = (t % GRID_W).astype(jnp.float32)
    quarter = dim // 4
    inv = ROPE_BASE ** (-jnp.arange(quarter, dtype=jnp.float32) / quarter)
    ang_r = row[:, None] * inv[None]
    ang_c = col[:, None] * inv[None]
    ang = jnp.concatenate([ang_r, ang_r, ang_c, ang_c], axis=-1)
    return jnp.cos(ang), jnp.sin(ang)


def apply_rope(x, cos, sin):
    x1, x2, x3, x4 = jnp.split(x, 4, axis=-1)
    rot = jnp.concatenate([-x2, x1, -x4, x3], axis=-1)
    return x * cos[:, None] + rot * sin[:, None]


def neighbourhood_attention(q, k, v, qc, kc, vc, rpb, with_ctx_out):
    b, n, h, dh = q.shape
    rows = n // GRID_W
    kh = min(NA_WIN_ROWS, rows)
    scale = dh ** -0.5
    qg = q.reshape(b, rows, GRID_W, h, dh)
    kg = k.reshape(b, rows, GRID_W, h, dh)
    vg = v.reshape(b, rows, GRID_W, h, dh)
    r = jnp.arange(rows)
    row_start = jnp.clip(r - kh // 2, 0, rows - kh)
    row_idx = row_start[:, None] + jnp.arange(kh)[None, :]
    k_blk = kg[:, row_idx]
    v_blk = vg[:, row_idx]
    col = jnp.arange(GRID_W)
    col_start = jnp.clip(col - NA_WIN_COLS // 2, 0, GRID_W - NA_WIN_COLS)
    in_win = (col[None, :] >= col_start[:, None]) & (col[None, :] < col_start[:, None] + NA_WIN_COLS)
    dr = row_idx - r[:, None] + NA_WIN_ROWS - 1
    dc = jnp.clip(col[None, :] - col[:, None] + NA_WIN_COLS - 1, 0, 2 * NA_WIN_COLS - 2)
    bias = rpb.astype(jnp.float32)[:, dr[:, None, :, None], dc[None, :, None, :]]
    bias = jnp.where(in_win[:, None, :], bias, NEG_INF)
    s_lat = jnp.einsum('brqhd,brikhd->bhrqik', qg, k_blk).astype(jnp.float32) * scale + bias[None]
    s_ctx = jnp.einsum('brqhd,blhd->bhrql', qg, kc).astype(jnp.float32) * scale
    n_lat = kh * GRID_W
    s = jnp.concatenate([s_lat.reshape(b, h, rows, GRID_W, n_lat), s_ctx], axis=-1)
    p = jax.nn.softmax(s, axis=-1)
    p_lat = p[..., :n_lat].reshape(b, h, rows, GRID_W, kh, GRID_W).astype(v.dtype)
    p_ctx = p[..., n_lat:].astype(v.dtype)
    o = jnp.einsum('bhrqik,brikhd->brqhd', p_lat, v_blk) + jnp.einsum('bhrql,blhd->brqhd', p_ctx, vc)
    o = o.reshape(b, n, h, dh)
    o_ctx = None
    if with_ctx_out:
        sc = jnp.einsum('bqhd,bkhd->bhqk', qc, kc).astype(jnp.float32) * scale
        o_ctx = jnp.einsum('bhqk,bkhd->bqhd', jax.nn.softmax(sc, axis=-1).astype(vc.dtype), vc)
    return o, o_ctx


def linear_recurrence(a, bx, h0):
    def combine(l, r):
        return (l[0] * r[0], r[0] * l[1] + r[1])
    a_cum, b_cum = lax.associative_scan(combine, (a, bx), axis=1)
    h = a_cum * h0[:, None] + b_cum
    return h, h[:, -1]


def rglru_scan(inputs, h0, wa, ba, wx, bx, lam):
    (xs,) = inputs
    b, t, w = xs.shape
    xb = xs.reshape(b, t, LRU_BLOCKS, LRU_BLOCK)
    r_gate = jax.nn.sigmoid(jnp.einsum('btnc,ncd->btnd', xb, wa) + ba).reshape(b, t, w)
    i_gate = jax.nn.sigmoid(jnp.einsum('btnc,ncd->btnd', xb, wx) + bx).reshape(b, t, w)
    log_a = -LRU_C * r_gate * jax.nn.softplus(-lam.reshape(w))
    a = jnp.exp(log_a)
    gated_x = jnp.sqrt(-jnp.expm1(2.0 * log_a)) * (i_gate * xs)
    return linear_recurrence(a, gated_x, h0)


def to_chunks(a):
    b, t, h = a.shape[:3]
    return jnp.moveaxis(a.reshape(b, t // CHUNK, CHUNK, h, *a.shape[3:]), 3, 1)


def from_chunks(a):
    b, h, n, c, d = a.shape
    return jnp.moveaxis(a, 1, 3).reshape(b, n * c, h, d)


def gated_delta_chunked(inputs, s0):
    q, k, v, g, beta = (to_chunks(a) for a in inputs)
    dk = q.shape[-1]
    dv = v.shape[-1]
    q = q * dk ** -0.5
    gc = jnp.cumsum(g, axis=-1)
    incl = jnp.tril(jnp.ones((CHUNK, CHUNK), dtype=bool))
    strict = jnp.tril(jnp.ones((CHUNK, CHUNK), dtype=bool), -1)
    diff = gc[..., :, None] - gc[..., None, :]
    decay = jnp.where(incl, jnp.exp(jnp.where(incl, diff, 0.0)), 0.0)
    kb = k * beta[..., None]
    a_low = jnp.where(strict, jnp.einsum('bhncd,bhnsd->bhncs', kb, k) * decay, 0.0)
    rhs = jnp.concatenate([v * beta[..., None], kb * jnp.exp(gc)[..., None]], axis=-1)
    sol = lax.linalg.triangular_solve(a_low + jnp.eye(CHUNK, dtype=a_low.dtype), rhs,
                                      left_side=True, lower=True, unit_diagonal=True)
    u, w = sol[..., :dv], sol[..., dv:]
    attn = jnp.where(incl, jnp.einsum('bhncd,bhnsd->bhncs', q, k) * decay, 0.0)
    q_dec = q * jnp.exp(gc)[..., None]
    k_dec = k * jnp.exp(gc[..., -1:] - gc)[..., None]
    g_last = jnp.exp(gc[..., -1])

    def step(s, xs):
        qd, kd, uu, ww, at, gl = xs
        v_new = uu - jnp.einsum('bhcd,bhde->bhce', ww, s)
        o = jnp.einsum('bhcd,bhde->bhce', qd, s) + jnp.einsum('bhcs,bhse->bhce', at, v_new)
        s = s * gl[..., None, None] + jnp.einsum('bhcd,bhce->bhde', kd, v_new)
        return s, o

    xs = tuple(jnp.moveaxis(a, 2, 0) for a in (q_dec, k_dec, u, w, attn, g_last))
    s_fin, o = lax.scan(step, s0, xs)
    return from_chunks(jnp.moveaxis(o, 0, 2)), s_fin


def retention_chunked(inputs, s0, log_gamma):
    q, k, v = (to_chunks(a) for a in inputs)
    k = k * k.shape[-1] ** -0.5
    pos = jnp.arange(CHUNK, dtype=jnp.float32)
    rel = pos[:, None] - pos[None, :]
    dmask = jnp.where(rel >= 0, jnp.exp(jnp.maximum(rel, 0.0)[None] * log_gamma[:, None, None]), 0.0)
    scores = jnp.einsum('bhncd,bhnsd->bhncs', q, k) * dmask[:, None]
    o_intra = jnp.einsum('bhncs,bhnse->bhnce', scores, v)
    lg = log_gamma[:, None]
    k_dec = k * jnp.exp((CHUNK - 1 - pos)[None] * lg)[:, None, :, None]
    q_dec = q * jnp.exp((pos + 1.0)[None] * lg)[:, None, :, None]
    chunk_kv = jnp.einsum('bhncd,bhnce->bhnde', k_dec, v)
    g_chunk = jnp.exp(CHUNK * log_gamma)[:, None, None]

    def step(s, xs):
        qd, kv = xs
        o = jnp.einsum('bhcd,bhde->bhce', qd, s)
        return s * g_chunk + kv, o

    s_fin, o_inter = lax.scan(step, s0, (jnp.moveaxis(q_dec, 2, 0), jnp.moveaxis(chunk_kv, 2, 0)))
    return from_chunks(o_intra + jnp.moveaxis(o_inter, 0, 2)), s_fin


def bidirectional_scan(fwd_fn, bwd_fn, lat_fwd, lat_bwd, ctx_fwd, ctx_bwd, h_zero):
    def rev(xs):
        return tuple(jnp.flip(a, axis=1) for a in xs)
    oc_f, hc_f = fwd_fn(ctx_fwd, h_zero)
    ol_f, _ = fwd_fn(lat_fwd, hc_f)
    oc_b, hc_b = bwd_fn(rev(ctx_bwd), h_zero)
    ol_b, _ = bwd_fn(rev(lat_bwd), hc_b)
    return ol_f + jnp.flip(ol_b, axis=1), oc_f + jnp.flip(oc_b, axis=1)


def hybrid_layer(x, ctx, mod_x, mod_c, norm_g, w_in, na_rpb, lru_conv_w, lru_conv_b, lru_wa, lru_ba,
                 lru_wx, lru_bx, lru_lam, gdn_conv_w, gdn_a_log, gdn_dt_bias, gdn_norm_g, w_branch,
                 w_out, rope_cos, rope_sin, log_gamma, with_ctx_out):
    f32 = jnp.float32
    bsz, n, _ = x.shape
    n_ctx = ctx.shape[1]
    shift_x, scale_x, gate_x = jnp.split(mod_x[:, None, :], 3, axis=-1)
    shift_c, scale_c, gate_c = jnp.split(mod_c, 3, axis=-1)
    px = split_columns((rmsnorm(x, norm_g) * (1.0 + scale_x) + shift_x) @ w_in)
    pc = split_columns((rmsnorm(ctx, norm_g) * (1.0 + scale_c) + shift_c) @ w_in)

    o_na = neighbourhood_attention(
        heads(px['na_q'], NA_HEADS), heads(px['na_k'], NA_HEADS), heads(px['na_v'], NA_HEADS),
        heads(pc['na_q'], NA_HEADS), heads(pc['na_k'], NA_HEADS), heads(pc['na_v'], NA_HEADS),
        na_rpb, with_ctx_out)

    lx = (dwconv(px['lru_x'], lru_conv_w) + lru_conv_b).astype(f32)
    lc = (dwconv(pc['lru_x'], lru_conv_w) + lru_conv_b).astype(f32)
    lru_f = functools.partial(rglru_scan, wa=lru_wa[0], ba=lru_ba[0], wx=lru_wx[0], bx=lru_bx[0], lam=lru_lam[0])
    lru_b = functools.partial(rglru_scan, wa=lru_wa[1], ba=lru_ba[1], wx=lru_wx[1], bx=lru_bx[1], lam=lru_lam[1])
    o_lru = bidirectional_scan(lru_f, lru_b, (lx,), (lx,), (lc,), (lc,), jnp.zeros((bsz, LRU_WIDTH), f32))

    def gdn_inputs(p):
        qkv = jax.nn.silu(dwconv(p['gdn_qkv'], gdn_conv_w)).astype(f32)
        q, k, v = (heads(a, GDN_HEADS) for a in jnp.split(qkv, 3, axis=-1))
        q, k = l2norm(q), l2norm(k)
        b_f, b_b, a_f, a_b = jnp.split(p['gdn_ab'].astype(f32), 4, axis=-1)
        g_f = -jnp.exp(gdn_a_log[0]) * jax.nn.softplus(a_f + gdn_dt_bias[0])
        g_b = -jnp.exp(gdn_a_log[1]) * jax.nn.softplus(a_b + gdn_dt_bias[1])
        return (q, k, v, g_f, jax.nn.sigmoid(b_f)), (q, k, v, g_b, jax.nn.sigmoid(b_b))
    gx_f, gx_b = gdn_inputs(px)
    gc_f, gc_b = gdn_inputs(pc)
    o_gdn = bidirectional_scan(gated_delta_chunked, gated_delta_chunked, gx_f, gx_b, gc_f, gc_b,
                               jnp.zeros((bsz, GDN_HEADS, GDN_HEAD_DIM, GDN_HEAD_DIM), f32))

    def ret_inputs(p):
        return tuple(heads(p[name], RET_HEADS).astype(f32) for name in ('ret_q', 'ret_k', 'ret_v'))
    rq, rk, rv = ret_inputs(px)
    rx = (apply_rope(rq, rope_cos, rope_sin), apply_rope(rk, rope_cos, rope_sin), rv)
    rc = ret_inputs(pc)
    ret_fn = functools.partial(retention_chunked, log_gamma=log_gamma)
    o_ret = bidirectional_scan(ret_fn, ret_fn, rx, rx, rc, rc,
                               jnp.zeros((bsz, RET_HEADS, RET_QK_DIM, RET_V_DIM), f32))

    def merge(p, oa, ob, oc, od, length):
        silu = jax.nn.silu
        ya = oa.reshape(bsz, length, NA_WIDTH) * silu(p['na_z'])
        yb = ob * silu(p['lru_z'])
        yc = (rms(oc) * gdn_norm_g * silu(heads(p['gdn_z'], GDN_HEADS))).reshape(bsz, length, GDN_WIDTH)
        yd = (rms(od) * silu(heads(p['ret_z'], RET_HEADS))).reshape(bsz, length, RET_WIDTH)
        br = jnp.stack([ya, yb, yc, yd], axis=2)
        y = jnp.einsum('btiw,iwd->btid', br, w_branch)
        gates = jax.nn.sigmoid(p['merge']).reshape(bsz, length, N_BRANCH, D_MODEL)
        return jnp.sum(gates * y, axis=2) @ w_out

    x_new = x + gate_x * merge(px, o_na[0], o_lru[0], o_gdn[0], o_ret[0], n)
    ctx_new = None
    if with_ctx_out:
        ctx_new = ctx + gate_c * merge(pc, o_na[1], o_lru[1], o_gdn[1], o_ret[1], n_ctx)
    return x_new, ctx_new


def setup_inputs(seed: int = 0) -> dict:
    key = jax.random.key(seed)
    ks = jax.random.split(key, 24)
    f32 = jnp.float32
    D = D_MODEL

    def nrm(k, shape, s):
        return jax.random.normal(k, shape, f32) * s

    a0 = jax.random.uniform(ks[15], (DEPTH, 2, LRU_BLOCKS, LRU_BLOCK), f32, 0.9, 0.999)
    sig = a0 ** (1.0 / LRU_C)
    dt = jnp.exp(jax.random.uniform(ks[18], (DEPTH, 2, GDN_HEADS), f32, math.log(1e-3), math.log(1e-1)))
    return {
        'x': nrm(ks[0], (BATCH, SEQ, D), 1.0),
        'c': nrm(ks[1], (BATCH, D), 1.0),
        'ctx': nrm(ks[2], (BATCH, CTX_LEN, D), 1.0),
        'c_ctx': nrm(ks[3], (D,), 1.0),
        'norm_g': 1.0 + nrm(ks[4], (DEPTH, D), 0.02),
        'w_mod': nrm(ks[5], (DEPTH, D, 3 * D), 0.5 * D ** -0.5),
        'b_mod': nrm(ks[6], (DEPTH, 3 * D), 0.01),
        'w_in': nrm(ks[7], (DEPTH, D, D_IN), D ** -0.5),
        'na_rpb': nrm(ks[8], (DEPTH, NA_HEADS, 2 * NA_WIN_ROWS - 1, 2 * NA_WIN_COLS - 1), 0.02),
        'lru_conv_w': nrm(ks[9], (DEPTH, CONV_W, LRU_WIDTH), CONV_W ** -0.5),
        'lru_conv_b': nrm(ks[10], (DEPTH, LRU_WIDTH), 0.01),
        'lru_wa': nrm(ks[11], (DEPTH, 2, LRU_BLOCKS, LRU_BLOCK, LRU_BLOCK), LRU_BLOCK ** -0.5),
        'lru_ba': nrm(ks[12], (DEPTH, 2, LRU_BLOCKS, LRU_BLOCK), 0.01),
        'lru_wx': nrm(ks[13], (DEPTH, 2, LRU_BLOCKS, LRU_BLOCK, LRU_BLOCK), LRU_BLOCK ** -0.5),
        'lru_bx': nrm(ks[14], (DEPTH, 2, LRU_BLOCKS, LRU_BLOCK), 0.01),
        'lru_lam': jnp.log(sig) - jnp.log1p(-sig),
        'gdn_conv_w': nrm(ks[16], (DEPTH, CONV_W, 3 * GDN_WIDTH), CONV_W ** -0.5),
        'gdn_a_log': jnp.log(jax.random.uniform(ks[17], (DEPTH, 2, GDN_HEADS), f32, 1.0, 16.0)),
        'gdn_dt_bias': dt + jnp.log(-jnp.expm1(-dt)),
        'gdn_norm_g': 1.0 + nrm(ks[19], (DEPTH, GDN_HEAD_DIM), 0.02),
        'w_branch': nrm(ks[20], (DEPTH, N_BRANCH, BRANCH_WIDTH, D), BRANCH_WIDTH ** -0.5),
        'w_out': nrm(ks[21], (DEPTH, D, D), D ** -0.5),
        'final_norm_g': 1.0 + nrm(ks[22], (D,), 0.02),
    }


def reference(x, c, ctx, c_ctx, norm_g, w_mod, b_mod, w_in, na_rpb, lru_conv_w, lru_conv_b, lru_wa,
              lru_ba, lru_wx, lru_bx, lru_lam, gdn_conv_w, gdn_a_log, gdn_dt_bias, gdn_norm_g,
              w_branch, w_out, final_norm_g):
    n = x.shape[1]
    rope_cos, rope_sin = axial_rope(n, RET_QK_DIM)
    log_gamma = jnp.log1p(-jnp.exp2(-(5.0 + jnp.arange(RET_HEADS, dtype=jnp.float32))))
    c_act = jax.nn.silu(c)
    cc_act = jax.nn.silu(c_ctx)
    for layer in range(DEPTH):
        mod_x = c_act @ w_mod[layer] + b_mod[layer]
        mod_c = cc_act @ w_mod[layer] + b_mod[layer]
        x, ctx = hybrid_layer(x, ctx, mod_x, mod_c, norm_g[layer], w_in[layer], na_rpb[layer],
                              lru_conv_w[layer], lru_conv_b[layer], lru_wa[layer], lru_ba[layer],
                              lru_wx[layer], lru_bx[layer], lru_lam[layer], gdn_conv_w[layer],
                              gdn_a_log[layer], gdn_dt_bias[layer], gdn_norm_g[layer], w_branch[layer],
                              w_out[layer], rope_cos, rope_sin, log_gamma, layer < DEPTH - 1)
    return rmsnorm(x, final_norm_g)
```

```python
import functools
import math

import numpy as np
import jax
import jax.numpy as jnp
from jax import lax
from jax.experimental import pallas as pl
from jax.experimental.pallas import tpu as pltpu

F32 = jnp.float32
BF16 = jnp.bfloat16

D_MODEL = 1024
GRID_W = 64
BRANCH = 512
N_BRANCH = 4
NA_HEADS = 8
NA_DH = 64
NA_WIN_ROWS = 8
NA_WIN_COLS = 16
LRU_BLOCKS = 8
LRU_BLOCK = 64
LRU_C = 8.0
CONV_W = 4
GDN_HEADS = 4
GDN_DH = 128
RET_HEADS = 4
RET_QK = 64
RET_V = 128
CHUNK = 64
ROPE_BASE = 10000.0
NORM_EPS = 1e-6
L2_EPS = 1e-6
NEG_INF = -1e30

V7X_LANES = 128
V7X_SUBLANES = 8
VMEM_LIMIT = 56 * 1024 * 1024

COL_MERGE = 0
COL_Z = 4096
COL_GDN = 6144
COL_NAQ = 7680
COL_NAK = 8192
COL_NAV = 8704
COL_LRU = 9216
COL_RETQ = 9728
COL_RETK = 9984
COL_RETV = 10240
U_COLS = 10752
IN_TN = 1792

_SRC = {}
_off = 0
for _name, _w in (('na_q', 512), ('na_k', 512), ('na_v', 512), ('na_z', 512), ('lru_x', 512),
                  ('lru_z', 512), ('gdn_qkv', 1536), ('gdn_ab', 16), ('gdn_z', 512), ('ret_q', 256),
                  ('ret_k', 256), ('ret_v', 512), ('ret_z', 512), ('merge', 4096)):
    _SRC[_name] = (_off, _w)
    _off += _w
D_IN = _off


def _cparams(sem):
    return pltpu.CompilerParams(dimension_semantics=sem, vmem_limit_bytes=VMEM_LIMIT)


def _softplus(x):
    return jnp.maximum(x, 0.0) + jnp.log1p(jnp.exp(-jnp.abs(x)))


def _sigmoid(x):
    return 1.0 / (1.0 + jnp.exp(-x))


def _silu(x):
    return x * _sigmoid(x)


def _mod_kernel(c_ref, w_ref, b_ref, o_ref):
    c = c_ref[...]
    a = _silu(c).astype(BF16)
    o_ref[...] = jnp.dot(a, w_ref[...].astype(BF16), preferred_element_type=F32) + b_ref[...]


def _modulation(cc, w_mod, b_mod):
    depth = w_mod.shape[0]
    rows = cc.shape[0]
    tn = 512
    return pl.pallas_call(
        _mod_kernel,
        grid=(depth, 3 * D_MODEL // tn),
        in_specs=[pl.BlockSpec((rows, D_MODEL), lambda l, j: (0, 0)),
                  pl.BlockSpec((None, D_MODEL, tn), lambda l, j: (l, 0, j)),
                  pl.BlockSpec((None, 1, tn), lambda l, j: (l, 0, j))],
        out_specs=pl.BlockSpec((None, rows, tn), lambda l, j: (l, 0, j)),
        out_shape=jax.ShapeDtypeStruct((depth, rows, 3 * D_MODEL), F32),
        compiler_params=_cparams(("arbitrary", "arbitrary")),
        name="modulation",
    )(cc, w_mod, b_mod.reshape(depth, 1, 3 * D_MODEL))


def _inproj_kernel(x_ref, g_ref, sh_ref, sc_ref, w_ref, wab_ref, u_ref, ab_ref, xn_ref):
    @pl.when(pl.program_id(1) == 0)
    def _():
        x = x_ref[...]
        ms = jnp.mean(x * x, axis=-1, keepdims=True)
        xn = x * lax.rsqrt(ms + NORM_EPS) * g_ref[...]
        xn = xn * (1.0 + sc_ref[...]) + sh_ref[...]
        xb = xn.astype(BF16)
        xn_ref[...] = xb
        ab_ref[...] = jnp.dot(xb, wab_ref[...], preferred_element_type=F32)

    u_ref[...] = jnp.dot(xn_ref[...], w_ref[...], preferred_element_type=F32).astype(BF16)


def _inproj(x2, g, shift, scale, w_p, w_ab, tm, tiles_per_mod):
    t = x2.shape[0]
    return pl.pallas_call(
        _inproj_kernel,
        grid=(t // tm, U_COLS // IN_TN),
        in_specs=[pl.BlockSpec((tm, D_MODEL), lambda i, j: (i, 0)),
                  pl.BlockSpec((1, D_MODEL), lambda i, j: (0, 0)),
                  pl.BlockSpec((None, 1, D_MODEL), lambda i, j: (i // tiles_per_mod, 0, 0)),
                  pl.BlockSpec((None, 1, D_MODEL), lambda i, j: (i // tiles_per_mod, 0, 0)),
                  pl.BlockSpec((D_MODEL, IN_TN), lambda i, j: (0, j)),
                  pl.BlockSpec((D_MODEL, V7X_LANES), lambda i, j: (0, 0))],
        out_specs=[pl.BlockSpec((tm, IN_TN), lambda i, j: (i, j)),
                   pl.BlockSpec((tm, V7X_LANES), lambda i, j: (i, 0))],
        out_shape=[jax.ShapeDtypeStruct((t, U_COLS), BF16),
                   jax.ShapeDtypeStruct((t, V7X_LANES), F32)],
        scratch_shapes=[pltpu.VMEM((tm, D_MODEL), BF16)],
        compiler_params=_cparams(("arbitrary", "arbitrary")),
        name="inproj",
    )(x2, g, shift, scale, w_p, w_ab)


def _na_kernel(rows, q_ref, k_ref, v_ref, kc_ref, vc_ref, t_ref, o_ref):
    i = pl.program_id(1)
    ws = jnp.clip(8 * i - 4, 0, rows - 16)
    lane = lax.broadcasted_iota(jnp.int32, (GRID_W, V7X_LANES), 1)
    lo_half = lane < NA_DH

    def row_body(a, carry):
        r = 8 * i + a
        row_start = jnp.clip(r - NA_WIN_ROWS // 2, 0, rows - NA_WIN_ROWS)
        off = pl.multiple_of((row_start - ws) * GRID_W, GRID_W)
        dstart = row_start - r + NA_WIN_ROWS - 1
        qoff = pl.multiple_of(a * GRID_W, GRID_W)
        for p in range(NA_HEADS // 2):
            ls = slice(p * V7X_LANES, (p + 1) * V7X_LANES)
            q_pair = q_ref[pl.ds(qoff, GRID_W), ls]
            q_pair = (q_pair.astype(F32) * (NA_DH ** -0.5)).astype(BF16)
            kw = k_ref[pl.ds(off, NA_WIN_ROWS * GRID_W), ls]
            vw = v_ref[pl.ds(off, NA_WIN_ROWS * GRID_W), ls]
            kc = kc_ref[:, ls]
            vc = vc_ref[:, ls]
            outs = []
            for e in range(2):
                sel = lo_half if e == 0 else jnp.logical_not(lo_half)
                qm = jnp.where(sel, q_pair, jnp.zeros_like(q_pair))
                s_lat = lax.dot_general(qm, kw, (((1,), (1,)), ((), ())), preferred_element_type=F32)
                s_lat = s_lat + t_ref[dstart, 2 * p + e]
                s_ctx = lax.dot_general(qm, kc, (((1,), (1,)), ((), ())), preferred_element_type=F32)
                m = jnp.maximum(jnp.max(s_lat, axis=-1, keepdims=True), jnp.max(s_ctx, axis=-1, keepdims=True))
                e_lat = jnp.exp(s_lat - m)
                e_ctx = jnp.exp(s_ctx - m)
                den = jnp.sum(e_lat, axis=-1, keepdims=True) + jnp.sum(e_ctx, axis=-1, keepdims=True)
                o = jnp.dot(e_lat.astype(BF16), vw, preferred_element_type=F32)
                o = o + jnp.dot(e_ctx.astype(BF16), vc, preferred_element_type=F32)
                outs.append(o / den)
            o_pair = jnp.where(lo_half, outs[0], outs[1])
            o_ref[pl.ds(qoff, GRID_W), ls] = o_pair.astype(BF16)
        return carry

    lax.fori_loop(0, 8, row_body, 0)


def _na_latent(u, uc, tvar, bsz, n, n_ctx):
    rows = n // GRID_W
    assert rows >= 16 and rows % 8 == 0
    nblk = rows // 8
    tq = 8 * GRID_W
    win = 16 * GRID_W
    u3 = u.reshape(bsz, n, U_COLS)

    def kv_spec(col):
        return pl.BlockSpec((None, pl.Element(win), pl.Element(BRANCH)),
                            lambda b, i: (b, jnp.clip(8 * i - 4, 0, rows - 16) * GRID_W, col))

    return pl.pallas_call(
        functools.partial(_na_kernel, rows),
        grid=(bsz, nblk),
        in_specs=[pl.BlockSpec((tq, BRANCH), lambda b, i: (b * nblk + i, COL_NAQ // BRANCH)),
                  kv_spec(COL_NAK), kv_spec(COL_NAV),
                  pl.BlockSpec((n_ctx, BRANCH), lambda b, i: (b, COL_NAK // BRANCH)),
                  pl.BlockSpec((n_ctx, BRANCH), lambda b, i: (b, COL_NAV // BRANCH)),
                  pl.BlockSpec(tvar.shape, lambda b, i: (0, 0, 0, 0), pipeline_mode=pl.Buffered(1))],
        out_specs=pl.BlockSpec((tq, BRANCH), lambda b, i: (b * nblk + i, 0)),
        out_shape=jax.ShapeDtypeStruct((bsz * n, BRANCH), BF16),
        compiler_params=_cparams(("arbitrary", "arbitrary")),
        name="na_latent",
    )(u, u3, u3, uc, uc, tvar)


def _na_ctx_kernel(q_ref, k_ref, v_ref, o_ref):
    n_ctx = q_ref.shape[0]
    lane = lax.broadcasted_iota(jnp.int32, (n_ctx, V7X_LANES), 1)
    lo_half = lane < NA_DH
    for p in range(NA_HEADS // 2):
        ls = slice(p * V7X_LANES, (p + 1) * V7X_LANES)
        q_pair = (q_ref[:, ls].astype(F32) * (NA_DH ** -0.5)).astype(BF16)
        k = k_ref[:, ls]
        v = v_ref[:, ls]
        outs = []
        for e in range(2):
            sel = lo_half if e == 0 else jnp.logical_not(lo_half)
            qm = jnp.where(sel, q_pair, jnp.zeros_like(q_pair))
            s = lax.dot_general(qm, k, (((1,), (1,)), ((), ())), preferred_element_type=F32)
            m = jnp.max(s, axis=-1, keepdims=True)
            ex = jnp.exp(s - m)
            den = jnp.sum(ex, axis=-1, keepdims=True)
            outs.append(jnp.dot(ex.astype(BF16), v, preferred_element_type=F32) / den)
        o_ref[:, ls] = jnp.where(lo_half, outs[0], outs[1]).astype(BF16)


def _na_ctx(uc, bsz, n_ctx):
    def spec(col):
        return pl.BlockSpec((n_ctx, BRANCH), lambda b: (b, col // BRANCH))
    return pl.pallas_call(
        _na_ctx_kernel,
        grid=(bsz,),
        in_specs=[spec(COL_NAQ), spec(COL_NAK), spec(COL_NAV)],
        out_specs=pl.BlockSpec((n_ctx, BRANCH), lambda b: (b, 0)),
        out_shape=jax.ShapeDtypeStruct((bsz * n_ctx, BRANCH), BF16),
        compiler_params=_cparams(("arbitrary",)),
        name="na_ctx",
    )(uc, uc, uc)


def _dwconv(x, prev8, next8, w):
    tt = x.shape[0]
    w0, w1, w2, w3 = (w[j:j + 1] for j in range(CONV_W))

    def taps(a):
        n = a.shape[0]
        return (w0 * pltpu.roll(a, 2, 0) + w1 * pltpu.roll(a, 1, 0) + w2 * a
                + w3 * pltpu.roll(a, n - 1, 0))

    y = taps(x)
    head = taps(jnp.concatenate([prev8, x[0:16]], axis=0))[8:16]
    tail = taps(jnp.concatenate([x[tt - 16:tt], next8], axis=0))[8:16]
    return jnp.concatenate([head, y[8:tt - 8], tail], axis=0)


def _tile_index(rev):
    i = pl.program_id(1)
    nt = pl.num_programs(1)
    ti = (nt - 1 - i) if rev else i
    return i, ti, nt


def _scan_specs(tt, nt, bsz, rev, col, width):
    r8 = tt // V7X_SUBLANES
    cb = col // width
    last8 = bsz * nt * r8 - 1

    def tix(i):
        return (nt - 1 - i) if rev else i

    main = pl.BlockSpec((tt, width), lambda b, i: (b * nt + tix(i), cb))
    prev = pl.BlockSpec((V7X_SUBLANES, width),
                        lambda b, i: (jnp.maximum((b * nt + tix(i)) * r8 - 1, 0), cb))
    nxt = pl.BlockSpec((V7X_SUBLANES, width),
                       lambda b, i: (jnp.minimum((b * nt + tix(i) + 1) * r8, last8), cb))
    return main, prev, nxt


def _halo(ref, is_edge):
    h = ref[...].astype(F32)
    return jnp.where(is_edge, jnp.zeros_like(h), h)


def _lru_kernel(rev, x_ref, xp_ref, xn_ref, cw_ref, cb_ref, wg_ref, ba_ref, bx_ref, lam_ref, h0_ref,
                o_ref, hf_ref, a_s, b_s, h_s):
    i, ti, nt = _tile_index(rev)
    tt = x_ref.shape[0]
    ng = tt // V7X_SUBLANES

    @pl.when(i == 0)
    def _():
        h_s[...] = h0_ref[...]

    xs = _dwconv(x_ref[...].astype(F32), _halo(xp_ref, ti == 0), _halo(xn_ref, ti == nt - 1),
                 cw_ref[...]) + cb_ref[...]
    xb = xs.astype(BF16)
    pa, px = [], []
    for c in range(BRANCH // V7X_LANES):
        g = jnp.dot(xb[:, c * V7X_LANES:(c + 1) * V7X_LANES], wg_ref[c], preferred_element_type=F32)
        pa.append(g[:, :V7X_LANES])
        px.append(g[:, V7X_LANES:])
    r_gate = _sigmoid(jnp.concatenate(pa, axis=1) + ba_ref[...])
    i_gate = _sigmoid(jnp.concatenate(px, axis=1) + bx_ref[...])
    z = LRU_C * r_gate * _softplus(-lam_ref[...])
    a = jnp.exp(-z)
    th = jnp.tanh(z)
    b = jnp.sqrt(2.0 * th / (1.0 + th)) * (i_gate * xs)

    row = lax.broadcasted_iota(jnp.int32, (tt, BRANCH), 0) & (V7X_SUBLANES - 1)
    for s in (1, 2, 4):
        if rev:
            a_sh, b_sh, valid = pltpu.roll(a, tt - s, 0), pltpu.roll(b, tt - s, 0), row < V7X_SUBLANES - s
        else:
            a_sh, b_sh, valid = pltpu.roll(a, s, 0), pltpu.roll(b, s, 0), row >= s
        b = jnp.where(valid, b + a * b_sh, b)
        a = jnp.where(valid, a * a_sh, a)
    a_s[...] = a
    b_s[...] = b

    def body(j, h):
        jj = (ng - 1 - j) if rev else j
        r0 = pl.multiple_of(jj * V7X_SUBLANES, V7X_SUBLANES)
        hb = a_s[pl.ds(r0, V7X_SUBLANES), :] * h + b_s[pl.ds(r0, V7X_SUBLANES), :]
        b_s[pl.ds(r0, V7X_SUBLANES), :] = hb
        return hb[0:1] if rev else hb[V7X_SUBLANES - 1:V7X_SUBLANES]

    h = lax.fori_loop(0, ng, body, h_s[...], unroll=4)
    h_s[...] = h
    hf_ref[...] = h
    o_ref[...] = b_s[...].astype(BF16)


def _lru(u, rev, bsz, n, tt, cw, cb, wg, ba, bx, lam, h0):
    nt = n // tt
    main, prev, nxt = _scan_specs(tt, nt, bsz, rev, COL_LRU, BRANCH)

    def tix(i):
        return (nt - 1 - i) if rev else i

    full = lambda shape: pl.BlockSpec(shape, lambda b, i: (0,) * len(shape))
    return pl.pallas_call(
        functools.partial(_lru_kernel, rev),
        grid=(bsz, nt),
        in_specs=[main, prev, nxt, full((CONV_W, BRANCH)), full((1, BRANCH)),
                  full((BRANCH // V7X_LANES, V7X_LANES, 2 * V7X_LANES)),
                  full((1, BRANCH)), full((1, BRANCH)), full((1, BRANCH)),
                  pl.BlockSpec((None, 1, BRANCH), lambda b, i: (b, 0, 0))],
        out_specs=[pl.BlockSpec((tt, BRANCH), lambda b, i: (b * nt + tix(i), 0)),
                   pl.BlockSpec((None, 1, BRANCH), lambda b, i: (b, 0, 0))],
        out_shape=[jax.ShapeDtypeStruct((bsz * n, BRANCH), BF16),
                   jax.ShapeDtypeStruct((bsz, 1, BRANCH), F32)],
        scratch_shapes=[pltpu.VMEM((tt, BRANCH), F32), pltpu.VMEM((tt, BRANCH), F32),
                        pltpu.VMEM((1, BRANCH), F32)],
        compiler_params=_cparams(("arbitrary", "arbitrary")),
        name="lru_bwd" if rev else "lru_fwd",
    )(u, u, u, cw, cb, wg, ba, bx, lam, h0)


def _heads_l2norm(a):
    outs = []
    for h in range(GDN_HEADS):
        ah = a[:, h * GDN_DH:(h + 1) * GDN_DH]
        outs.append(ah * lax.rsqrt(jnp.sum(ah * ah, axis=-1, keepdims=True) + L2_EPS))
    return jnp.concatenate(outs, axis=1)


def _bdot(a, b):
    return jnp.einsum('ncs,nst->nct', a.astype(BF16), b.astype(BF16), preferred_element_type=F32)


def _gdn_kernel(rev, x_ref, xp_ref, xn_ref, ab_ref, cw_ref, alog_ref, dtb_ref, s0_ref,
                o_ref, sf_ref, qd_s, kdt_s, u_s, w_s, at_s, gl_s, o_s, s_s):
    i, ti, nt = _tile_index(rev)
    tt = x_ref.shape[0]
    nc = tt // CHUNK

    @pl.when(i == 0)
    def _():
        s_s[...] = s0_ref[...]

    qkv = _dwconv(x_ref[...].astype(F32), _halo(xp_ref, ti == 0), _halo(xn_ref, ti == nt - 1), cw_ref[...])
    qkv = _silu(qkv)
    q = _heads_l2norm(qkv[:, :BRANCH]) * (GDN_DH ** -0.5)
    k = _heads_l2norm(qkv[:, BRANCH:2 * BRANCH])
    v = qkv[:, 2 * BRANCH:]

    ab = ab_ref[...]
    boff = GDN_HEADS if rev else 0
    aoff = 3 * GDN_HEADS if rev else 2 * GDN_HEADS

    def head_cols(off):
        return jnp.concatenate(
            [jnp.broadcast_to(ab[:, off + h:off + h + 1], (tt, GDN_DH)) for h in range(GDN_HEADS)], axis=1)

    beta = _sigmoid(head_cols(boff))
    g = -jnp.exp(alog_ref[...]) * _softplus(head_cols(aoff) + dtb_ref[...])

    rowc = lax.broadcasted_iota(jnp.int32, (tt, BRANCH), 0) & (CHUNK - 1)
    gc = g
    for s in (1, 2, 4, 8, 16, 32):
        if rev:
            sh, valid = pltpu.roll(gc, tt - s, 0), rowc < CHUNK - s
        else:
            sh, valid = pltpu.roll(gc, s, 0), rowc >= s
        gc = gc + jnp.where(valid, sh, 0.0)
    gcr = gc.reshape(nc, CHUNK, BRANCH)
    glr = gcr[:, 0:1, :] if rev else gcr[:, CHUNK - 1:CHUNK, :]
    eg = jnp.exp(gc)
    kd = k * jnp.exp(glr - gcr).reshape(tt, BRANCH)
    kb = k * beta
    qd_s[...] = q * eg
    gl_s[...] = jnp.broadcast_to(jnp.exp(glr), (nc, V7X_SUBLANES, BRANCH))
    rhs_v = v * beta
    rhs_k = kb * eg

    ci = lax.broadcasted_iota(jnp.int32, (CHUNK, CHUNK), 0)
    si = lax.broadcasted_iota(jnp.int32, (CHUNK, CHUNK), 1)
    incl = (ci <= si) if rev else (ci >= si)
    strict = (ci < si) if rev else (ci > si)

    for h in range(GDN_HEADS):
        ls = slice(h * GDN_DH, (h + 1) * GDN_DH)
        k_h = k[:, ls].reshape(nc, CHUNK, GDN_DH).astype(BF16)
        lhs = jnp.concatenate([kb[:, ls].reshape(nc, CHUNK, GDN_DH), q[:, ls].reshape(nc, CHUNK, GDN_DH)],
                              axis=1).astype(BF16)
        kkqk = jnp.einsum('ncd,nsd->ncs', lhs, k_h, preferred_element_type=F32)
        gch = gcr[:, :, ls]
        gcol = gch[:, :, :CHUNK]
        grow = jnp.stack([gch[n].T[:CHUNK] for n in range(nc)], axis=0)
        diff = gcol - grow
        decay = jnp.where(incl, jnp.exp(jnp.where(incl, diff, 0.0)), 0.0)
        a_low = jnp.where(strict, kkqk[:, :CHUNK] * decay, 0.0)
        attn = jnp.where(incl, kkqk[:, CHUNK:] * decay, 0.0)
        nmat = -a_low
        m = a_low
        for _ in range(5):
            m = _bdot(m, m)
            nmat = nmat + m + _bdot(nmat, m)
        rhs = jnp.concatenate([rhs_v[:, ls].reshape(nc, CHUNK, GDN_DH),
                               rhs_k[:, ls].reshape(nc, CHUNK, GDN_DH)], axis=2)
        sol = rhs + _bdot(nmat, rhs)
        u_s[:, ls] = sol[:, :, :GDN_DH].reshape(tt, GDN_DH)
        w_s[:, ls] = sol[:, :, GDN_DH:].reshape(tt, GDN_DH)
        kd_h = kd[:, ls].reshape(nc, CHUNK, GDN_DH)
        for n in range(nc):
            kdt_s[n * GDN_HEADS + h] = kd_h[n].T
            at_s[n * GDN_HEADS + h] = attn[n]

    def chunk_body(j, carry):
        cidx = (nc - 1 - j) if rev else j
        r0 = pl.multiple_of(cidx * CHUNK, CHUNK)
        for h in range(GDN_HEADS):
            ls = slice(h * GDN_DH, (h + 1) * GDN_DH)
            s_h = s_s[ls, :]
            s_b = s_h.astype(BF16)
            v_new = u_s[pl.ds(r0, CHUNK), ls] - jnp.dot(w_s[pl.ds(r0, CHUNK), ls].astype(BF16), s_b,
                                                       preferred_element_type=F32)
            v_b = v_new.astype(BF16)
            o_h = jnp.dot(qd_s[pl.ds(r0, CHUNK), ls].astype(BF16), s_b, preferred_element_type=F32)
            o_h = o_h + jnp.dot(at_s[cidx * GDN_HEADS + h].astype(BF16), v_b, preferred_element_type=F32)
            o_s[pl.ds(r0, CHUNK), ls] = o_h
            gl = gl_s[cidx][0:1, ls]
            s_s[ls, :] = s_h * gl + jnp.dot(kdt_s[cidx * GDN_HEADS + h].astype(BF16), v_b,
                                            preferred_element_type=F32)
        return carry

    lax.fori_loop(0, nc, chunk_body, 0)
    o_ref[...] = o_s[...].astype(BF16)
    sf_ref[...] = s_s[...]


def _gdn(u, ab, rev, bsz, n, tt, cw, alog_b, dtb_b, s0):
    nt = n // tt
    nc = tt // CHUNK
    main, prev, nxt = _scan_specs(tt, nt, bsz, rev, COL_GDN, 3 * BRANCH)

    def tix(i):
        return (nt - 1 - i) if rev else i

    full = lambda shape: pl.BlockSpec(shape, lambda b, i: (0,) * len(shape))
    state = pl.BlockSpec((None, GDN_HEADS * GDN_DH, GDN_DH), lambda b, i: (b, 0, 0))
    return pl.pallas_call(
        functools.partial(_gdn_kernel, rev),
        grid=(bsz, nt),
        in_specs=[main, prev, nxt,
                  pl.BlockSpec((tt, V7X_LANES), lambda b, i: (b * nt + tix(i), 0)),
                  full((CONV_W, 3 * BRANCH)), full((1, BRANCH)), full((1, BRANCH)), state],
        out_specs=[pl.BlockSpec((tt, BRANCH), lambda b, i: (b * nt + tix(i), 0)), state],
        out_shape=[jax.ShapeDtypeStruct((bsz * n, BRANCH), BF16),
                   jax.ShapeDtypeStruct((bsz, GDN_HEADS * GDN_DH, GDN_DH), F32)],
        scratch_shapes=[pltpu.VMEM((tt, BRANCH), F32),
                        pltpu.VMEM((nc * GDN_HEADS, GDN_DH, CHUNK), F32),
                        pltpu.VMEM((tt, BRANCH), F32),
                        pltpu.VMEM((tt, BRANCH), F32),
                        pltpu.VMEM((nc * GDN_HEADS, CHUNK, CHUNK), F32),
                        pltpu.VMEM((nc, V7X_SUBLANES, BRANCH), F32),
                        pltpu.VMEM((tt, BRANCH), F32),
                        pltpu.VMEM((GDN_HEADS * GDN_DH, GDN_DH), F32)],
        compiler_params=_cparams(("arbitrary", "arbitrary")),
        name="gdn_bwd" if rev else "gdn_fwd",
    )(u, u, u, ab, cw, alog_b, dtb_b, s0)


def _rope(x, cos, sin_signed):
    lane = lax.broadcasted_iota(jnp.int32, x.shape, 1)
    first = (lane & 31) < 16
    w = x.shape[1]
    rot = jnp.where(first, pltpu.roll(x, w - 16, 1), pltpu.roll(x, 16, 1))
    return x * cos + rot * sin_signed


def _ret_kernel(rev, rope, *refs):
    if rope:
        (q_ref, k_ref, v_ref, cos_ref, sin_ref, dm_ref, qsc_ref, ksc_ref, gch_ref, s0_ref,
         o_ref, sf_ref, s_s) = refs
    else:
        (q_ref, k_ref, v_ref, dm_ref, qsc_ref, ksc_ref, gch_ref, s0_ref, o_ref, sf_ref, s_s) = refs
    i = pl.program_id(1)
    tt = q_ref.shape[0]
    nc = tt // CHUNK

    @pl.when(i == 0)
    def _():
        s_s[...] = s0_ref[...]

    q = q_ref[...].astype(F32)
    k = k_ref[...].astype(F32)
    if rope:
        q = _rope(q, cos_ref[...], sin_ref[...])
        k = _rope(k, cos_ref[...], sin_ref[...])
    k = k * (RET_QK ** -0.5)
    q3 = q.reshape(nc, CHUNK, RET_HEADS * RET_QK)
    k3 = k.reshape(nc, CHUNK, RET_HEADS * RET_QK)
    qd3 = q3 * qsc_ref[...]
    kd3 = k3 * ksc_ref[...]
    v = v_ref[...]
    lane = lax.broadcasted_iota(jnp.int32, (nc, CHUNK, V7X_LANES), 1 + 1)
    lo_half = lane < RET_QK

    for h in range(RET_HEADS):
        p, e = h // 2, h % 2
        ps = slice(p * V7X_LANES, (p + 1) * V7X_LANES)
        vs = slice(h * RET_V, (h + 1) * RET_V)
        sel = lo_half if e == 0 else jnp.logical_not(lo_half)
        q_h = jnp.where(sel, q3[:, :, ps], 0.0).astype(BF16)
        k_p = k3[:, :, ps].astype(BF16)
        v_h = v[:, vs].reshape(nc, CHUNK, RET_V)
        scores = jnp.einsum('ncd,nsd->ncs', q_h, k_p, preferred_element_type=F32) * dm_ref[h]
        o_intra = _bdot(scores, v_h)
        kd_h = jnp.where(sel, kd3[:, :, ps], 0.0)
        qd_h = jnp.where(sel, qd3[:, :, ps], 0.0).astype(BF16)
        gch = gch_ref[0:1, vs]
        s_h = s_s[h * V7X_LANES:(h + 1) * V7X_LANES, :]
        order = range(nc - 1, -1, -1) if rev else range(nc)
        outs = [None] * nc
        for n in order:
            outs[n] = o_intra[n] + jnp.dot(qd_h[n], s_h.astype(BF16), preferred_element_type=F32)
            kv = jnp.dot(kd_h[n].T.astype(BF16), v_h[n], preferred_element_type=F32)
            s_h = s_h * gch + kv
        s_s[h * V7X_LANES:(h + 1) * V7X_LANES, :] = s_h
        o_ref[:, vs] = jnp.concatenate(outs, axis=0).astype(BF16)
    sf_ref[...] = s_s[...]


def _ret(u, rev, bsz, n, tt, tabs, rope_tabs, s0):
    nt = n // tt
    dm, qsc, ksc, gch = tabs

    def tix(i):
        return (nt - 1 - i) if rev else i

    qk_w = RET_HEADS * RET_QK
    full = lambda shape: pl.BlockSpec(shape, lambda b, i: (0,) * len(shape))
    state = pl.BlockSpec((None, RET_HEADS * V7X_LANES, RET_V), lambda b, i: (b, 0, 0))
    in_specs = [pl.BlockSpec((tt, qk_w), lambda b, i: (b * nt + tix(i), COL_RETQ // qk_w)),
                pl.BlockSpec((tt, qk_w), lambda b, i: (b * nt + tix(i), COL_RETK // qk_w)),
                pl.BlockSpec((tt, BRANCH), lambda b, i: (b * nt + tix(i), COL_RETV // BRANCH))]
    args = [u, u, u]
    if rope_tabs is not None:
        in_specs += [pl.BlockSpec((tt, qk_w), lambda b, i: (tix(i), 0))] * 2
        args += list(rope_tabs)
    in_specs += [full(dm.shape), full(qsc.shape), full(ksc.shape), full(gch.shape), state]
    args += [dm, qsc, ksc, gch, s0]
    return pl.pallas_call(
        functools.partial(_ret_kernel, rev, rope_tabs is not None),
        grid=(bsz, nt),
        in_specs=in_specs,
        out_specs=[pl.BlockSpec((tt, BRANCH), lambda b, i: (b * nt + tix(i), 0)), state],
        out_shape=[jax.ShapeDtypeStruct((bsz * n, BRANCH), BF16),
                   jax.ShapeDtypeStruct((bsz, RET_HEADS * V7X_LANES, RET_V), F32)],
        scratch_shapes=[pltpu.VMEM((RET_HEADS * V7X_LANES, RET_V), F32)],
        compiler_params=_cparams(("arbitrary", "arbitrary")),
        name="ret_bwd" if rev else "ret_fwd",
    )(*args)


def _head_rms(a):
    outs = []
    for h in range(BRANCH // V7X_LANES):
        ah = a[:, h * V7X_LANES:(h + 1) * V7X_LANES]
        outs.append(ah * lax.rsqrt(jnp.mean(ah * ah, axis=-1, keepdims=True) + NORM_EPS))
    return jnp.concatenate(outs, axis=1)


def _merge_kernel(final, ona_ref, lf_ref, lb_ref, gf_ref, gb_ref, rf_ref, rb_ref, z_ref, mg_ref, x_ref,
                  gate_ref, gng_ref, wb_ref, wo_ref, fg_ref, o_ref):
    z = z_ref[...].astype(F32)
    sz = _silu(z)
    ya = ona_ref[...].astype(F32) * sz[:, :BRANCH]
    yb = (lf_ref[...].astype(F32) + lb_ref[...].astype(F32)) * sz[:, BRANCH:2 * BRANCH]
    oc = gf_ref[...].astype(F32) + gb_ref[...].astype(F32)
    yc = _head_rms(oc) * gng_ref[...] * sz[:, 2 * BRANCH:3 * BRANCH]
    od = rf_ref[...].astype(F32) + rb_ref[...].astype(F32)
    yd = _head_rms(od) * sz[:, 3 * BRANCH:]
    acc = None
    for idx, y in enumerate((ya, yb, yc, yd)):
        proj = jnp.dot(y.astype(BF16), wb_ref[idx], preferred_element_type=F32)
        gate = _sigmoid(mg_ref[:, idx * D_MODEL:(idx + 1) * D_MODEL].astype(F32))
        acc = gate * proj if acc is None else acc + gate * proj
    out = jnp.dot(acc.astype(BF16), wo_ref[...], preferred_element_type=F32)
    xn = x_ref[...] + gate_ref[...] * out
    if final:
        ms = jnp.mean(xn * xn, axis=-1, keepdims=True)
        xn = xn * lax.rsqrt(ms + NORM_EPS) * fg_ref[...]
    o_ref[...] = xn


def _merge(final, o_na, lru, gdn, ret, u, x2, gate, gng, wb, wo, fg, tm, tiles_per_mod):
    t = x2.shape[0]
    br = pl.BlockSpec((tm, BRANCH), lambda i: (i, 0))
    full = lambda shape: pl.BlockSpec(shape, lambda i: (0,) * len(shape))
    return pl.pallas_call(
        functools.partial(_merge_kernel, final),
        grid=(t // tm,),
        in_specs=[br] * 7 + [
            pl.BlockSpec((tm, 4 * BRANCH), lambda i: (i, COL_Z // (4 * BRANCH))),
            pl.BlockSpec((tm, N_BRANCH * D_MODEL), lambda i: (i, 0)),
            pl.BlockSpec((tm, D_MODEL), lambda i: (i, 0)),
            pl.BlockSpec((None, 1, D_MODEL), lambda i: (i // tiles_per_mod, 0, 0)),
            full((1, BRANCH)), full((N_BRANCH, BRANCH, D_MODEL)), full((D_MODEL, D_MODEL)),
            full((1, D_MODEL))],
        out_specs=pl.BlockSpec((tm, D_MODEL), lambda i: (i, 0)),
        out_shape=jax.ShapeDtypeStruct((t, D_MODEL), F32),
        compiler_params=_cparams(("arbitrary",)),
        name="merge",
    )(o_na, lru[0], lru[1], gdn[0], gdn[1], ret[0], ret[1], u, u, x2, gate, gng, wb, wo, fg)


def _permute_w_in(w):
    def cols(name):
        o, n = _SRC[name]
        return w[:, o:o + n]
    w_p = jnp.concatenate([cols('merge'), cols('na_z'), cols('lru_z'), cols('gdn_z'), cols('ret_z'),
                           cols('gdn_qkv'), cols('na_q'), cols('na_k'), cols('na_v'), cols('lru_x'),
                           cols('ret_q'), cols('ret_k'), cols('ret_v')], axis=1)
    w_ab = jnp.pad(cols('gdn_ab'), ((0, 0), (0, V7X_LANES - 4 * GDN_HEADS)))
    return w_p.astype(BF16), w_ab.astype(BF16)


def _na_bias_tables(rpb):
    col = np.arange(GRID_W)
    col_start = np.clip(col - NA_WIN_COLS // 2, 0, GRID_W - NA_WIN_COLS)
    in_win = (col[None, :] >= col_start[:, None]) & (col[None, :] < col_start[:, None] + NA_WIN_COLS)
    dc = np.clip(col[None, :] - col[:, None] + NA_WIN_COLS - 1, 0, 2 * NA_WIN_COLS - 2)
    t = rpb.astype(F32)[:, :, dc]
    t = jnp.where(in_win[None, None], t, NEG_INF)
    idx = np.arange(NA_WIN_ROWS)[:, None] + np.arange(NA_WIN_ROWS)[None, :]
    tv = t[:, idx]
    return jnp.transpose(tv, (1, 0, 3, 2, 4)).reshape(NA_WIN_ROWS, NA_HEADS, GRID_W, NA_WIN_ROWS * GRID_W)


def _lru_gate_weights(wa, wx):
    def chunk(w, c):
        z = jnp.zeros((LRU_BLOCK, LRU_BLOCK), w.dtype)
        return jnp.concatenate([jnp.concatenate([w[2 * c], z], axis=1),
                                jnp.concatenate([z, w[2 * c + 1]], axis=1)], axis=0)
    return jnp.stack([jnp.concatenate([chunk(wa, c), chunk(wx, c)], axis=1)
                      for c in range(LRU_BLOCKS // 2)], axis=0).astype(BF16)


def _rope_tables(n):
    t = np.arange(n)
    row = (t // GRID_W).astype(np.float32)
    col = (t % GRID_W).astype(np.float32)
    quarter = RET_QK // 4
    inv = jnp.asarray(ROPE_BASE, F32) ** (-jnp.arange(quarter, dtype=F32) / quarter)
    ang_r = jnp.asarray(row)[:, None] * inv[None]
    ang_c = jnp.asarray(col)[:, None] * inv[None]
    ang = jnp.concatenate([ang_r, ang_r, ang_c, ang_c], axis=-1)
    cos, sin = jnp.cos(ang), jnp.sin(ang)
    sign = np.where((np.arange(RET_QK) % 32) < 16, -1.0, 1.0).astype(np.float32)
    return jnp.tile(cos, (1, RET_HEADS)), jnp.tile(sin * sign[None], (1, RET_HEADS))


def _ret_tables(rev):
    log_gamma = jnp.log1p(-jnp.exp2(-(5.0 + jnp.arange(RET_HEADS, dtype=F32))))
    pos = jnp.arange(CHUNK, dtype=F32)
    if rev:
        pos = pos[::-1]
    rel = pos[:, None] - pos[None, :]
    dm = jnp.where(rel >= 0, jnp.exp(jnp.maximum(rel, 0.0)[None] * log_gamma[:, None, None]), 0.0)
    ksc = jnp.exp((CHUNK - 1 - pos)[None] * log_gamma[:, None])
    qsc = jnp.exp((pos + 1.0)[None] * log_gamma[:, None])
    widen = lambda a: jnp.repeat(a.T, RET_QK, axis=1)
    gch = jnp.broadcast_to(jnp.repeat(jnp.exp(CHUNK * log_gamma), RET_V)[None], (V7X_SUBLANES, RET_HEADS * RET_V))
    return dm, widen(qsc), widen(ksc), gch


def _lane_rows(vals, width):
    return jnp.repeat(vals.astype(F32), width)[None]


def _layer(x2, c2, bsz, n, n_ctx, mod, p, consts, with_ctx_out, final):
    shift_x = mod[:bsz, None, :D_MODEL]
    scale_x = mod[:bsz, None, D_MODEL:2 * D_MODEL]
    gate_x = mod[:bsz, None, 2 * D_MODEL:]
    shift_c = mod[bsz:bsz + 1, None, :D_MODEL]
    scale_c = mod[bsz:bsz + 1, None, D_MODEL:2 * D_MODEL]
    gate_c = mod[bsz:bsz + 1, None, 2 * D_MODEL:]

    tm = 1024 if n % 1024 == 0 else 512
    tmc = min(1024, bsz * n_ctx)
    tt = 512
    big = 1 << 30
    u, ab = _inproj(x2, p['norm_g'], shift_x, scale_x, p['w_p'], p['w_ab'], tm, n // tm)
    uc, abc = _inproj(c2, p['norm_g'], shift_c, scale_c, p['w_p'], p['w_ab'], tmc, big)

    o_na = _na_latent(u, uc, p['na_tab'], bsz, n, n_ctx)

    lru_o, lru_oc = [], []
    gdn_o, gdn_oc = [], []
    ret_o, ret_oc = [], []
    for d, rev in enumerate((False, True)):
        h0 = jnp.zeros((bsz, 1, BRANCH), F32)
        lru_args = (p['lru_cw'], p['lru_cb'], p['lru_wg'][d], p['lru_ba'][d], p['lru_bx'][d], p['lru_lam'][d])
        oc_, hc = _lru(uc, rev, bsz, n_ctx, n_ctx, *lru_args, h0)
        ol_, _ = _lru(u, rev, bsz, n, tt, *lru_args, hc)
        lru_o.append(ol_)
        lru_oc.append(oc_)

        s0 = jnp.zeros((bsz, GDN_HEADS * GDN_DH, GDN_DH), F32)
        gdn_args = (p['gdn_cw'], p['gdn_alog'][d], p['gdn_dtb'][d])
        oc_, sc = _gdn(uc, abc, rev, bsz, n_ctx, n_ctx, *gdn_args, s0)
        ol_, _ = _gdn(u, ab, rev, bsz, n, tt, *gdn_args, sc)
        gdn_o.append(ol_)
        gdn_oc.append(oc_)

        r0 = jnp.zeros((bsz, RET_HEADS * V7X_LANES, RET_V), F32)
        oc_, rc = _ret(uc, rev, bsz, n_ctx, n_ctx, consts['ret_tabs'][d], None, r0)
        ol_, _ = _ret(u, rev, bsz, n, tt, consts['ret_tabs'][d], consts['rope'], rc)
        ret_o.append(ol_)
        ret_oc.append(oc_)

    merge_w = (p['gdn_ng'], p['w_branch'], p['w_out'], consts['final_g'])
    x_new = _merge(final, o_na, lru_o, gdn_o, ret_o, u, x2, gate_x, *merge_w, 512, n // 512)
    c_new = None
    if with_ctx_out:
        o_nac = _na_ctx(uc, bsz, n_ctx)
        c_new = _merge(False, o_nac, lru_oc, gdn_oc, ret_oc, uc, c2, gate_c, *merge_w, n_ctx, big)
    return x_new, c_new


def kernel(x, c, ctx, c_ctx, norm_g, w_mod, b_mod, w_in, na_rpb, lru_conv_w, lru_conv_b, lru_wa, lru_ba,
           lru_wx, lru_bx, lru_lam, gdn_conv_w, gdn_a_log, gdn_dt_bias, gdn_norm_g, w_branch, w_out,
           final_norm_g):
    bsz, n, d = x.shape
    n_ctx = ctx.shape[1]
    depth = w_in.shape[0]
    assert d == D_MODEL and w_in.shape[2] == D_IN
    assert n % 512 == 0 and n_ctx % CHUNK == 0 and (bsz * n_ctx) % 8 == 0

    rows = -(-(bsz + 1) // V7X_SUBLANES) * V7X_SUBLANES
    cc = jnp.zeros((rows, D_MODEL), F32).at[:bsz].set(c).at[bsz].set(c_ctx)
    mods = _modulation(cc, w_mod, b_mod)

    consts = {
        'rope': _rope_tables(n),
        'ret_tabs': (_ret_tables(False), _ret_tables(True)),
        'final_g': final_norm_g.reshape(1, D_MODEL),
    }

    x2 = x.reshape(bsz * n, D_MODEL)
    c2 = ctx.reshape(bsz * n_ctx, D_MODEL)
    for layer in range(depth):
        w_p, w_ab = _permute_w_in(w_in[layer])
        p = {
            'norm_g': norm_g[layer].reshape(1, D_MODEL),
            'w_p': w_p, 'w_ab': w_ab,
            'na_tab': _na_bias_tables(na_rpb[layer]),
            'lru_cw': lru_conv_w[layer], 'lru_cb': lru_conv_b[layer].reshape(1, BRANCH),
            'lru_wg': [_lru_gate_weights(lru_wa[layer, dd], lru_wx[layer, dd]) for dd in range(2)],
            'lru_ba': [lru_ba[layer, dd].reshape(1, BRANCH) for dd in range(2)],
            'lru_bx': [lru_bx[layer, dd].reshape(1, BRANCH) for dd in range(2)],
            'lru_lam': [lru_lam[layer, dd].reshape(1, BRANCH) for dd in range(2)],
            'gdn_cw': gdn_conv_w[layer],
            'gdn_alog': [_lane_rows(gdn_a_log[layer, dd], GDN_DH) for dd in range(2)],
            'gdn_dtb': [_lane_rows(gdn_dt_bias[layer, dd], GDN_DH) for dd in range(2)],
            'gdn_ng': jnp.tile(gdn_norm_g[layer].astype(F32), GDN_HEADS)[None],
            'w_branch': w_branch[layer].astype(BF16),
            'w_out': w_out[layer].astype(BF16),
        }
        last = layer == depth - 1
        x2, c2 = _layer(x2, c2, bsz, n, n_ctx, mods[layer], p, consts, not last, last)
    return x2.reshape(bsz, n, D_MODEL)
```

```python
import functools
import math

import numpy as np
import jax
import jax.numpy as jnp
from jax import lax
from jax.experimental import pallas as pl
from jax.experimental.pallas import tpu as pltpu

F32 = jnp.float32
BF16 = jnp.bfloat16

D_MODEL = 1024
GRID_W = 64
BRANCH = 512
N_BRANCH = 4
NA_HEADS = 8
NA_DH = 64
NA_WIN_ROWS = 8
NA_WIN_COLS = 16
LRU_BLOCKS = 8
LRU_BLOCK = 64
LRU_C = 8.0
CONV_W = 4
GDN_HEADS = 4
GDN_DH = 128
RET_HEADS = 4
RET_QK = 64
RET_V = 128
CHUNK = 64
LOG2_CHUNK = 6
assert 1 << LOG2_CHUNK == CHUNK
RET_CHUNK = 128
ROPE_BASE = 10000.0
NORM_EPS = 1e-6
L2_EPS = 1e-6
NEG_INF = -1e30

V7X_LANES = 128
V7X_SUBLANES = 8
VMEM_LIMIT = 56 * 1024 * 1024

COL_MERGE = 0
COL_Z = 4096
COL_GDN = 6144
COL_NAQ = 7680
COL_NAK = 8192
COL_NAV = 8704
COL_LRU = 9216
COL_RETQ = 9728
COL_RETK = 9984
COL_RETV = 10240
U_COLS = 10752
IN_TN = 1792

_SRC = {}
_off = 0
for _name, _w in (('na_q', 512), ('na_k', 512), ('na_v', 512), ('na_z', 512), ('lru_x', 512),
                  ('lru_z', 512), ('gdn_qkv', 1536), ('gdn_ab', 16), ('gdn_z', 512), ('ret_q', 256),
                  ('ret_k', 256), ('ret_v', 512), ('ret_z', 512), ('merge', 4096)):
    _SRC[_name] = (_off, _w)
    _off += _w
D_IN = _off


def _cparams(sem):
    return pltpu.CompilerParams(dimension_semantics=sem, vmem_limit_bytes=VMEM_LIMIT)


def _softplus(x):
    return jnp.maximum(x, 0.0) + jnp.log1p(jnp.exp(-jnp.abs(x)))


def _sigmoid(x):
    return 1.0 / (1.0 + jnp.exp(-x))


def _silu(x):
    return x * _sigmoid(x)


def _mod_kernel(c_ref, w_ref, b_ref, o_ref):
    c = c_ref[...]
    a = _silu(c).astype(BF16)
    o_ref[...] = jnp.dot(a, w_ref[...].astype(BF16), preferred_element_type=F32) + b_ref[...]


def _modulation(cc, w_mod, b_mod):
    depth = w_mod.shape[0]
    rows = cc.shape[0]
    tn = 512
    return pl.pallas_call(
        _mod_kernel,
        grid=(depth, 3 * D_MODEL // tn),
        in_specs=[pl.BlockSpec((rows, D_MODEL), lambda l, j: (0, 0)),
                  pl.BlockSpec((None, D_MODEL, tn), lambda l, j: (l, 0, j)),
                  pl.BlockSpec((None, 1, tn), lambda l, j: (l, 0, j))],
        out_specs=pl.BlockSpec((None, rows, tn), lambda l, j: (l, 0, j)),
        out_shape=jax.ShapeDtypeStruct((depth, rows, 3 * D_MODEL), F32),
        compiler_params=_cparams(("arbitrary", "arbitrary")),
        name="modulation",
    )(cc, w_mod, b_mod.reshape(depth, 1, 3 * D_MODEL))


def _inproj_kernel(x_ref, g_ref, sh_ref, sc_ref, w_ref, wab_ref, u_ref, ab_ref, xn_ref):
    @pl.when(pl.program_id(1) == 0)
    def _():
        x = x_ref[...]
        ms = jnp.mean(x * x, axis=-1, keepdims=True)
        xn = x * lax.rsqrt(ms + NORM_EPS) * g_ref[...]
        xn = xn * (1.0 + sc_ref[...]) + sh_ref[...]
        xb = xn.astype(BF16)
        xn_ref[...] = xb
        ab_ref[...] = jnp.dot(xb, wab_ref[...], preferred_element_type=F32)

    u_ref[...] = jnp.dot(xn_ref[...], w_ref[...], preferred_element_type=F32).astype(BF16)


def _inproj(x2, g, shift, scale, w_p, w_ab, tm, tiles_per_mod):
    t = x2.shape[0]
    return pl.pallas_call(
        _inproj_kernel,
        grid=(t // tm, U_COLS // IN_TN),
        in_specs=[pl.BlockSpec((tm, D_MODEL), lambda i, j: (i, 0)),
                  pl.BlockSpec((1, D_MODEL), lambda i, j: (0, 0)),
                  pl.BlockSpec((None, 1, D_MODEL), lambda i, j: (i // tiles_per_mod, 0, 0)),
                  pl.BlockSpec((None, 1, D_MODEL), lambda i, j: (i // tiles_per_mod, 0, 0)),
                  pl.BlockSpec((D_MODEL, IN_TN), lambda i, j: (0, j)),
                  pl.BlockSpec((D_MODEL, V7X_LANES), lambda i, j: (0, 0))],
        out_specs=[pl.BlockSpec((tm, IN_TN), lambda i, j: (i, j)),
                   pl.BlockSpec((tm, V7X_LANES), lambda i, j: (i, 0))],
        out_shape=[jax.ShapeDtypeStruct((t, U_COLS), BF16),
                   jax.ShapeDtypeStruct((t, V7X_LANES), F32)],
        scratch_shapes=[pltpu.VMEM((tm, D_MODEL), BF16)],
        compiler_params=_cparams(("arbitrary", "arbitrary")),
        name="inproj",
    )(x2, g, shift, scale, w_p, w_ab)


NA_GROUP_ROWS = 4
NA_GROUP_KEYS = (2 * (NA_GROUP_ROWS - 1) + NA_WIN_ROWS) * GRID_W


def _na_softmax_pv(qm, kw, vw, kc, vc, bias):
    s_lat = lax.dot_general(qm, kw, (((1,), (1,)), ((), ())), preferred_element_type=F32) + bias
    s_ctx = lax.dot_general(qm, kc, (((1,), (1,)), ((), ())), preferred_element_type=F32)
    m = jnp.maximum(jnp.max(s_lat, axis=-1, keepdims=True), jnp.max(s_ctx, axis=-1, keepdims=True))
    e_lat = jnp.exp(s_lat - m)
    e_ctx = jnp.exp(s_ctx - m)
    den = jnp.sum(e_lat, axis=-1, keepdims=True) + jnp.sum(e_ctx, axis=-1, keepdims=True)
    o = jnp.dot(e_lat.astype(BF16), vw, preferred_element_type=F32)
    o = o + jnp.dot(e_ctx.astype(BF16), vc, preferred_element_type=F32)
    return o / den


def _na_interior_block(q_ref, k_ref, v_ref, kc_ref, vc_ref, bint_ref, o_ref):
    gq = NA_GROUP_ROWS * GRID_W
    lane = lax.broadcasted_iota(jnp.int32, (gq, V7X_LANES), 1)
    lo_half = lane < NA_DH
    for p in range(NA_HEADS // 2):
        ls = slice(p * V7X_LANES, (p + 1) * V7X_LANES)
        kc = kc_ref[:, ls]
        vc = vc_ref[:, ls]
        for par in range(2):
            q_g = jnp.concatenate(
                [q_ref[(2 * m + par) * GRID_W:(2 * m + par + 1) * GRID_W, ls] for m in range(NA_GROUP_ROWS)], axis=0)
            q_g = (q_g.astype(F32) * (NA_DH ** -0.5)).astype(BF16)
            kw = k_ref[par * GRID_W:par * GRID_W + NA_GROUP_KEYS, ls]
            vw = v_ref[par * GRID_W:par * GRID_W + NA_GROUP_KEYS, ls]
            outs = []
            for e in range(2):
                sel = lo_half if e == 0 else jnp.logical_not(lo_half)
                qm = jnp.where(sel, q_g, jnp.zeros_like(q_g))
                outs.append(_na_softmax_pv(qm, kw, vw, kc, vc, bint_ref[2 * p + e]))
            o_pair = jnp.where(lo_half, outs[0], outs[1]).astype(BF16)
            for m in range(NA_GROUP_ROWS):
                o_ref[(2 * m + par) * GRID_W:(2 * m + par + 1) * GRID_W, ls] = o_pair[m * GRID_W:(m + 1) * GRID_W]


def _na_kernel(rows, q_ref, k_ref, v_ref, kc_ref, vc_ref, t_ref, bint_ref, o_ref):
    i = pl.program_id(1)
    interior = jnp.logical_and(i >= 1, i <= rows // 8 - 2)

    @pl.when(interior)
    def _():
        _na_interior_block(q_ref, k_ref, v_ref, kc_ref, vc_ref, bint_ref, o_ref)

    @pl.when(jnp.logical_not(interior))
    def _():
        _na_edge_block(rows, q_ref, k_ref, v_ref, kc_ref, vc_ref, t_ref, o_ref)


def _na_edge_block(rows, q_ref, k_ref, v_ref, kc_ref, vc_ref, t_ref, o_ref):
    i = pl.program_id(1)
    ws = jnp.clip(8 * i - 4, 0, rows - 16)
    lane = lax.broadcasted_iota(jnp.int32, (GRID_W, V7X_LANES), 1)
    lo_half = lane < NA_DH

    def row_body(a, carry):
        r = 8 * i + a
        row_start = jnp.clip(r - NA_WIN_ROWS // 2, 0, rows - NA_WIN_ROWS)
        off = pl.multiple_of((row_start - ws) * GRID_W, GRID_W)
        dstart = row_start - r + NA_WIN_ROWS - 1
        qoff = pl.multiple_of(a * GRID_W, GRID_W)
        for p in range(NA_HEADS // 2):
            ls = slice(p * V7X_LANES, (p + 1) * V7X_LANES)
            q_pair = q_ref[pl.ds(qoff, GRID_W), ls]
            q_pair = (q_pair.astype(F32) * (NA_DH ** -0.5)).astype(BF16)
            kw = k_ref[pl.ds(off, NA_WIN_ROWS * GRID_W), ls]
            vw = v_ref[pl.ds(off, NA_WIN_ROWS * GRID_W), ls]
            kc = kc_ref[:, ls]
            vc = vc_ref[:, ls]
            outs = []
            for e in range(2):
                sel = lo_half if e == 0 else jnp.logical_not(lo_half)
                qm = jnp.where(sel, q_pair, jnp.zeros_like(q_pair))
                outs.append(_na_softmax_pv(qm, kw, vw, kc, vc, t_ref[dstart, 2 * p + e]))
            o_pair = jnp.where(lo_half, outs[0], outs[1])
            o_ref[pl.ds(qoff, GRID_W), ls] = o_pair.astype(BF16)
        return carry

    lax.fori_loop(0, 8, row_body, 0)


def _na_latent(u, uc, tvar, bint, bsz, n, n_ctx):
    rows = n // GRID_W
    assert rows >= 16 and rows % 8 == 0
    nblk = rows // 8
    tq = 8 * GRID_W
    win = 16 * GRID_W
    u3 = u.reshape(bsz, n, U_COLS)

    def kv_spec(col):
        return pl.BlockSpec((None, pl.Element(win), pl.Element(BRANCH)),
                            lambda b, i: (b, jnp.clip(8 * i - 4, 0, rows - 16) * GRID_W, col))

    return pl.pallas_call(
        functools.partial(_na_kernel, rows),
        grid=(bsz, nblk),
        in_specs=[pl.BlockSpec((tq, BRANCH), lambda b, i: (b * nblk + i, COL_NAQ // BRANCH)),
                  kv_spec(COL_NAK), kv_spec(COL_NAV),
                  pl.BlockSpec((n_ctx, BRANCH), lambda b, i: (b, COL_NAK // BRANCH)),
                  pl.BlockSpec((n_ctx, BRANCH), lambda b, i: (b, COL_NAV // BRANCH)),
                  pl.BlockSpec(tvar.shape, lambda b, i: (0, 0, 0, 0), pipeline_mode=pl.Buffered(1)),
                  pl.BlockSpec(bint.shape, lambda b, i: (0, 0, 0), pipeline_mode=pl.Buffered(1))],
        out_specs=pl.BlockSpec((tq, BRANCH), lambda b, i: (b * nblk + i, 0)),
        out_shape=jax.ShapeDtypeStruct((bsz * n, BRANCH), BF16),
        compiler_params=_cparams(("arbitrary", "arbitrary")),
        name="na_latent",
    )(u, u3, u3, uc, uc, tvar, bint)


def _na_ctx_kernel(q_ref, k_ref, v_ref, o_ref):
    n_ctx = q_ref.shape[0]
    lane = lax.broadcasted_iota(jnp.int32, (n_ctx, V7X_LANES), 1)
    lo_half = lane < NA_DH
    for p in range(NA_HEADS // 2):
        ls = slice(p * V7X_LANES, (p + 1) * V7X_LANES)
        q_pair = (q_ref[:, ls].astype(F32) * (NA_DH ** -0.5)).astype(BF16)
        k = k_ref[:, ls]
        v = v_ref[:, ls]
        outs = []
        for e in range(2):
            sel = lo_half if e == 0 else jnp.logical_not(lo_half)
            qm = jnp.where(sel, q_pair, jnp.zeros_like(q_pair))
            s = lax.dot_general(qm, k, (((1,), (1,)), ((), ())), preferred_element_type=F32)
            m = jnp.max(s, axis=-1, keepdims=True)
            ex = jnp.exp(s - m)
            den = jnp.sum(ex, axis=-1, keepdims=True)
            outs.append(jnp.dot(ex.astype(BF16), v, preferred_element_type=F32) / den)
        o_ref[:, ls] = jnp.where(lo_half, outs[0], outs[1]).astype(BF16)


def _na_ctx(uc, bsz, n_ctx):
    def spec(col):
        return pl.BlockSpec((n_ctx, BRANCH), lambda b: (b, col // BRANCH))
    return pl.pallas_call(
        _na_ctx_kernel,
        grid=(bsz,),
        in_specs=[spec(COL_NAQ), spec(COL_NAK), spec(COL_NAV)],
        out_specs=pl.BlockSpec((n_ctx, BRANCH), lambda b: (b, 0)),
        out_shape=jax.ShapeDtypeStruct((bsz * n_ctx, BRANCH), BF16),
        compiler_params=_cparams(("arbitrary",)),
        name="na_ctx",
    )(uc, uc, uc)


def _dwconv(x, prev8, next8, w):
    tt = x.shape[0]
    w0, w1, w2, w3 = (w[j:j + 1] for j in range(CONV_W))

    def taps(a):
        n = a.shape[0]
        return (w0 * pltpu.roll(a, 2, 0) + w1 * pltpu.roll(a, 1, 0) + w2 * a
                + w3 * pltpu.roll(a, n - 1, 0))

    y = taps(x)
    head = taps(jnp.concatenate([prev8, x[0:16]], axis=0))[8:16]
    tail = taps(jnp.concatenate([x[tt - 16:tt], next8], axis=0))[8:16]
    return jnp.concatenate([head, y[8:tt - 8], tail], axis=0)


def _tile_index(rev):
    i = pl.program_id(1)
    nt = pl.num_programs(1)
    ti = (nt - 1 - i) if rev else i
    return i, ti, nt


def _scan_specs(tt, nt, bsz, rev, col, width):
    r8 = tt // V7X_SUBLANES
    cb = col // width
    last8 = bsz * nt * r8 - 1

    def tix(i):
        return (nt - 1 - i) if rev else i

    main = pl.BlockSpec((tt, width), lambda b, i: (b * nt + tix(i), cb))
    prev = pl.BlockSpec((V7X_SUBLANES, width),
                        lambda b, i: (jnp.maximum((b * nt + tix(i)) * r8 - 1, 0), cb))
    nxt = pl.BlockSpec((V7X_SUBLANES, width),
                       lambda b, i: (jnp.minimum((b * nt + tix(i) + 1) * r8, last8), cb))
    return main, prev, nxt


def _halo(ref, is_edge):
    h = ref[...].astype(F32)
    return jnp.where(is_edge, jnp.zeros_like(h), h)


def _lru_kernel(rev, x_ref, xp_ref, xn_ref, cw_ref, cb_ref, wg_ref, ba_ref, bx_ref, lam_ref, h0_ref,
                o_ref, hf_ref, a_s, b_s, h_s):
    i, ti, nt = _tile_index(rev)
    tt = x_ref.shape[0]
    ng = tt // V7X_SUBLANES

    @pl.when(i == 0)
    def _():
        h_s[...] = h0_ref[...]

    xs = _dwconv(x_ref[...].astype(F32), _halo(xp_ref, ti == 0), _halo(xn_ref, ti == nt - 1),
                 cw_ref[...]) + cb_ref[...]
    xb = xs.astype(BF16)
    pa, px = [], []
    for c in range(BRANCH // V7X_LANES):
        g = jnp.dot(xb[:, c * V7X_LANES:(c + 1) * V7X_LANES], wg_ref[c], preferred_element_type=F32)
        pa.append(g[:, :V7X_LANES])
        px.append(g[:, V7X_LANES:])
    r_gate = _sigmoid(jnp.concatenate(pa, axis=1) + ba_ref[...])
    i_gate = _sigmoid(jnp.concatenate(px, axis=1) + bx_ref[...])
    z = LRU_C * r_gate * _softplus(-lam_ref[...])
    a = jnp.exp(-z)
    th = jnp.tanh(z)
    b = jnp.sqrt(2.0 * th / (1.0 + th)) * (i_gate * xs)

    row = lax.broadcasted_iota(jnp.int32, (tt, BRANCH), 0) & (V7X_SUBLANES - 1)
    for s in (1, 2, 4):
        if rev:
            a_sh, b_sh, valid = pltpu.roll(a, tt - s, 0), pltpu.roll(b, tt - s, 0), row < V7X_SUBLANES - s
        else:
            a_sh, b_sh, valid = pltpu.roll(a, s, 0), pltpu.roll(b, s, 0), row >= s
        b = jnp.where(valid, b + a * b_sh, b)
        a = jnp.where(valid, a * a_sh, a)
    a_s[...] = a
    b_s[...] = b

    def body(j, h):
        jj = (ng - 1 - j) if rev else j
        r0 = pl.multiple_of(jj * V7X_SUBLANES, V7X_SUBLANES)
        hb = a_s[pl.ds(r0, V7X_SUBLANES), :] * h + b_s[pl.ds(r0, V7X_SUBLANES), :]
        b_s[pl.ds(r0, V7X_SUBLANES), :] = hb
        return hb[0:1] if rev else hb[V7X_SUBLANES - 1:V7X_SUBLANES]

    h = lax.fori_loop(0, ng, body, h_s[...], unroll=4)
    h_s[...] = h
    hf_ref[...] = h
    o_ref[...] = b_s[...].astype(BF16)


def _lru(u, rev, bsz, n, tt, cw, cb, wg, ba, bx, lam, h0):
    nt = n // tt
    main, prev, nxt = _scan_specs(tt, nt, bsz, rev, COL_LRU, BRANCH)

    def tix(i):
        return (nt - 1 - i) if rev else i

    full = lambda shape: pl.BlockSpec(shape, lambda b, i: (0,) * len(shape))
    return pl.pallas_call(
        functools.partial(_lru_kernel, rev),
        grid=(bsz, nt),
        in_specs=[main, prev, nxt, full((CONV_W, BRANCH)), full((1, BRANCH)),
                  full((BRANCH // V7X_LANES, V7X_LANES, 2 * V7X_LANES)),
                  full((1, BRANCH)), full((1, BRANCH)), full((1, BRANCH)),
                  pl.BlockSpec((None, 1, BRANCH), lambda b, i: (b, 0, 0))],
        out_specs=[pl.BlockSpec((tt, BRANCH), lambda b, i: (b * nt + tix(i), 0)),
                   pl.BlockSpec((None, 1, BRANCH), lambda b, i: (b, 0, 0))],
        out_shape=[jax.ShapeDtypeStruct((bsz * n, BRANCH), BF16),
                   jax.ShapeDtypeStruct((bsz, 1, BRANCH), F32)],
        scratch_shapes=[pltpu.VMEM((tt, BRANCH), F32), pltpu.VMEM((tt, BRANCH), F32),
                        pltpu.VMEM((1, BRANCH), F32)],
        compiler_params=_cparams(("arbitrary", "arbitrary")),
        name="lru_bwd" if rev else "lru_fwd",
    )(u, u, u, cw, cb, wg, ba, bx, lam, h0)


def _heads_l2norm(a):
    outs = []
    for h in range(GDN_HEADS):
        ah = a[:, h * GDN_DH:(h + 1) * GDN_DH]
        outs.append(ah * lax.rsqrt(jnp.sum(ah * ah, axis=-1, keepdims=True) + L2_EPS))
    return jnp.concatenate(outs, axis=1)


def _bdot(a, b):
    return jnp.einsum('ncs,nst->nct', a.astype(BF16), b.astype(BF16), preferred_element_type=F32)


def _gdn_prep_kernel(nt, x_ref, xp_ref, xn_ref, cw_ref, o_ref):
    ti = pl.program_id(0) % nt
    qkv = _dwconv(x_ref[...].astype(F32), _halo(xp_ref, ti == 0), _halo(xn_ref, ti == nt - 1), cw_ref[...])
    qkv = _silu(qkv)
    o_ref[:, :BRANCH] = (_heads_l2norm(qkv[:, :BRANCH]) * (GDN_DH ** -0.5)).astype(BF16)
    o_ref[:, BRANCH:2 * BRANCH] = _heads_l2norm(qkv[:, BRANCH:2 * BRANCH]).astype(BF16)
    o_ref[:, 2 * BRANCH:] = qkv[:, 2 * BRANCH:].astype(BF16)


def _gdn_prep(u, bsz, n, tt, cw):
    nt = n // tt
    r8 = tt // V7X_SUBLANES
    width = 3 * BRANCH
    cb = COL_GDN // width
    last8 = bsz * nt * r8 - 1
    return pl.pallas_call(
        functools.partial(_gdn_prep_kernel, nt),
        grid=(bsz * nt,),
        in_specs=[pl.BlockSpec((tt, width), lambda i: (i, cb)),
                  pl.BlockSpec((V7X_SUBLANES, width), lambda i: (jnp.maximum(i * r8 - 1, 0), cb)),
                  pl.BlockSpec((V7X_SUBLANES, width), lambda i: (jnp.minimum((i + 1) * r8, last8), cb)),
                  pl.BlockSpec((CONV_W, width), lambda i: (0, 0))],
        out_specs=pl.BlockSpec((tt, width), lambda i: (i, 0)),
        out_shape=jax.ShapeDtypeStruct((bsz * n, width), BF16),
        compiler_params=_cparams(("arbitrary",)),
        name="gdn_prep",
    )(u, u, u, cw)


def _gdn_kernel(rev, x_ref, ab_ref, alog_ref, dtb_ref, s0_ref,
                o_ref, sf_ref, kw_s, ku_s, qe_s, gl_s, o_s, s_s):
    i = pl.program_id(1)
    tt = x_ref.shape[0]
    nc = tt // CHUNK

    @pl.when(i == 0)
    def _():
        s_s[...] = s0_ref[...]

    q = x_ref[:, :BRANCH].astype(F32)
    k = x_ref[:, BRANCH:2 * BRANCH].astype(F32)
    v = x_ref[:, 2 * BRANCH:].astype(F32)

    ab = ab_ref[...]
    beta_c = _sigmoid(ab)
    gc = -jnp.exp(alog_ref[...]) * _softplus(ab + dtb_ref[...])
    rowc = lax.broadcasted_iota(jnp.int32, (tt, V7X_LANES), 0) & (CHUNK - 1)
    for s in (1 << b for b in range(LOG2_CHUNK)):
        if rev:
            sh, valid = pltpu.roll(gc, tt - s, 0), rowc < CHUNK - s
        else:
            sh, valid = pltpu.roll(gc, s, 0), rowc >= s
        gc = gc + jnp.where(valid, sh, 0.0)
    gcr_c = gc.reshape(nc, CHUNK, V7X_LANES)
    glr_c = gcr_c[:, 0:1, :] if rev else gcr_c[:, CHUNK - 1:CHUNK, :]
    ekd_c = jnp.exp(glr_c - gcr_c).reshape(tt, V7X_LANES)
    egl_c = jnp.broadcast_to(jnp.exp(glr_c), (nc, V7X_SUBLANES, V7X_LANES)).reshape(nc * V7X_SUBLANES, V7X_LANES)

    boff, aoff = _gdn_gate_cols(rev)

    def spread(a, off):
        return jnp.concatenate(
            [jnp.broadcast_to(a[:, off + h:off + h + 1], (a.shape[0], GDN_DH)) for h in range(GDN_HEADS)], axis=1)

    beta = spread(beta_c, boff)
    gcb = spread(gc, aoff)
    eg = spread(jnp.exp(gc), aoff)
    kd = k * spread(ekd_c, aoff)
    gl_s[...] = spread(egl_c, aoff).reshape(nc, V7X_SUBLANES, BRANCH)
    kb = k * beta
    qd = q * eg
    rhs_v = v * beta
    rhs_k = kb * eg

    ci = lax.broadcasted_iota(jnp.int32, (CHUNK, CHUNK), 0)
    si = lax.broadcasted_iota(jnp.int32, (CHUNK, CHUNK), 1)
    incl = (ci <= si) if rev else (ci >= si)
    strict = (ci < si) if rev else (ci > si)
    eye = (ci == si).astype(F32)
    same_blk = [(ci >> (3 + j)) == (si >> (3 + j)) for j in range(LOG2_CHUNK - 2)]

    nb = GDN_HEADS * nc

    def batched(a, col0=0):
        return jnp.concatenate(
            [a[:, col0 + h * GDN_DH:col0 + (h + 1) * GDN_DH].reshape(nc, CHUNK, GDN_DH) for h in range(GDN_HEADS)],
            axis=0)

    k_b = batched(x_ref, BRANCH)
    q_b = batched(x_ref)
    lhs = jnp.concatenate([batched(kb).astype(BF16), q_b], axis=1)
    kkqk = jnp.einsum('ncd,nsd->ncs', lhs, k_b, preferred_element_type=F32)
    gch = batched(gcb)
    gcol = gch[:, :, :CHUNK]
    grow = jnp.stack([gch[g].T[:CHUNK] for g in range(nb)], axis=0)
    diff = gcol - grow
    decay = jnp.where(incl, jnp.exp(jnp.where(incl, diff, 0.0)), 0.0)
    a_low = jnp.where(strict, kkqk[:, :CHUNK] * decay, 0.0)
    attn = jnp.where(incl, kkqk[:, CHUNK:] * decay, 0.0)
    d_blk = jnp.where(same_blk[0], a_low, 0.0)
    tinv = eye - d_blk
    m = d_blk
    for _ in range(2):
        m = _bdot(m, m)
        tinv = tinv + _bdot(tinv, m)
    for lvl in range(LOG2_CHUNK - 3):
        a_off = jnp.where(jnp.logical_and(same_blk[lvl + 1], jnp.logical_not(same_blk[lvl])), a_low, 0.0)
        tinv = tinv - _bdot(_bdot(tinv, a_off), tinv)
    rhs = jnp.concatenate([batched(rhs_v), batched(rhs_k)], axis=2)
    sol = rhs + _bdot(tinv - eye, rhs)
    kd_b = batched(kd)
    kdt = jnp.stack([kd_b[g].T for g in range(nb)], axis=0)
    m2 = _bdot(jnp.concatenate([kdt, attn], axis=1), sol)
    ku_s[...] = m2[:, :GDN_DH, :GDN_DH].reshape(GDN_HEADS, nc, GDN_DH, GDN_DH)
    kw_s[...] = m2[:, :GDN_DH, GDN_DH:].astype(BF16).reshape(GDN_HEADS, nc, GDN_DH, GDN_DH)
    qe = batched(qd) - m2[:, GDN_DH:, GDN_DH:]
    for h in range(GDN_HEADS):
        ls = slice(h * GDN_DH, (h + 1) * GDN_DH)
        o_s[:, ls] = m2[h * nc:(h + 1) * nc, GDN_DH:, :GDN_DH].reshape(tt, GDN_DH)
        qe_s[:, ls] = qe[h * nc:(h + 1) * nc].reshape(tt, GDN_DH).astype(BF16)

    def chunk_body(j, carry):
        cidx = (nc - 1 - j) if rev else j
        r0 = pl.multiple_of(cidx * CHUNK, CHUNK)
        for h in range(GDN_HEADS):
            ls = slice(h * GDN_DH, (h + 1) * GDN_DH)
            s_h = s_s[ls, :]
            s_b = s_h.astype(BF16)
            o_s[pl.ds(r0, CHUNK), ls] += jnp.dot(qe_s[pl.ds(r0, CHUNK), ls], s_b, preferred_element_type=F32)
            gl = gl_s[cidx][0:1, ls]
            s_s[ls, :] = (s_h * gl - jnp.dot(kw_s[h, cidx], s_b, preferred_element_type=F32)) + ku_s[h, cidx]
        return carry

    lax.fori_loop(0, nc, chunk_body, 0)
    o_ref[...] = o_s[...].astype(BF16)
    sf_ref[...] = s_s[...]


def _gdn(xn, ab, rev, bsz, n, tt, alog_c, dtb_c, s0):
    nt = n // tt
    nc = tt // CHUNK

    def tix(i):
        return (nt - 1 - i) if rev else i

    full = lambda shape: pl.BlockSpec(shape, lambda b, i: (0,) * len(shape))
    state = pl.BlockSpec((None, GDN_HEADS * GDN_DH, GDN_DH), lambda b, i: (b, 0, 0))
    return pl.pallas_call(
        functools.partial(_gdn_kernel, rev),
        grid=(bsz, nt),
        in_specs=[pl.BlockSpec((tt, 3 * BRANCH), lambda b, i: (b * nt + tix(i), 0)),
                  pl.BlockSpec((tt, V7X_LANES), lambda b, i: (b * nt + tix(i), 0)),
                  full((1, V7X_LANES)), full((1, V7X_LANES)), state],
        out_specs=[pl.BlockSpec((tt, BRANCH), lambda b, i: (b * nt + tix(i), 0)), state],
        out_shape=[jax.ShapeDtypeStruct((bsz * n, BRANCH), BF16),
                   jax.ShapeDtypeStruct((bsz, GDN_HEADS * GDN_DH, GDN_DH), F32)],
        scratch_shapes=[pltpu.VMEM((GDN_HEADS, nc, GDN_DH, GDN_DH), BF16),
                        pltpu.VMEM((GDN_HEADS, nc, GDN_DH, GDN_DH), F32),
                        pltpu.VMEM((tt, BRANCH), BF16),
                        pltpu.VMEM((nc, V7X_SUBLANES, BRANCH), F32),
                        pltpu.VMEM((tt, BRANCH), F32),
                        pltpu.VMEM((GDN_HEADS * GDN_DH, GDN_DH), F32)],
        compiler_params=_cparams(("arbitrary", "arbitrary")),
        name="gdn_bwd" if rev else "gdn_fwd",
    )(xn, ab, alog_c, dtb_c, s0)


def _rope(x, cos, sin_signed):
    lane = lax.broadcasted_iota(jnp.int32, x.shape, 1)
    first = (lane & 31) < 16
    w = x.shape[1]
    rot = jnp.where(first, pltpu.roll(x, w - 16, 1), pltpu.roll(x, 16, 1))
    return x * cos + rot * sin_signed


def _ret_kernel(rev, rope, *refs):
    if rope:
        (q_ref, k_ref, v_ref, cos_ref, sin_ref, dm_ref, qsc_ref, ksc_ref, gch_ref, s0_ref,
         o_ref, sf_ref, s_s) = refs
    else:
        (q_ref, k_ref, v_ref, dm_ref, qsc_ref, ksc_ref, gch_ref, s0_ref, o_ref, sf_ref, s_s) = refs
    i = pl.program_id(1)
    tt = q_ref.shape[0]
    nc = tt // RET_CHUNK

    @pl.when(i == 0)
    def _():
        s_s[...] = s0_ref[...]

    q = q_ref[...].astype(F32)
    k = k_ref[...].astype(F32)
    if rope:
        q = _rope(q, cos_ref[...], sin_ref[...])
        k = _rope(k, cos_ref[...], sin_ref[...])
    k = k * (RET_QK ** -0.5)
    q3 = q.reshape(nc, RET_CHUNK, RET_HEADS * RET_QK)
    k3 = k.reshape(nc, RET_CHUNK, RET_HEADS * RET_QK)
    qd3 = q3 * qsc_ref[...]
    kd3 = k3 * ksc_ref[...]
    v = v_ref[...]
    lane = lax.broadcasted_iota(jnp.int32, (nc, RET_CHUNK, V7X_LANES), 2)
    lo_half = lane < RET_QK

    def pair_lanes(a3, h, keep_other=False):
        pair = a3[:, :, (h // 2) * V7X_LANES:(h // 2 + 1) * V7X_LANES]
        if keep_other:
            return pair
        sel = lo_half if h % 2 == 0 else jnp.logical_not(lo_half)
        return jnp.where(sel, pair, 0.0)

    def batched(fn):
        return jnp.concatenate([fn(h) for h in range(RET_HEADS)], axis=0)

    nb = RET_HEADS * nc
    q_b = batched(lambda h: pair_lanes(q3, h)).astype(BF16)
    k_b = batched(lambda h: pair_lanes(k3, h, keep_other=True)).astype(BF16)
    v_b = batched(lambda h: v[:, h * RET_V:(h + 1) * RET_V].reshape(nc, RET_CHUNK, RET_V))
    dm_b = batched(lambda h: jnp.broadcast_to(dm_ref[h], (nc, RET_CHUNK, RET_CHUNK)))
    scores = jnp.einsum('ncd,nsd->ncs', q_b, k_b, preferred_element_type=F32) * dm_b
    o_intra = _bdot(scores, v_b)
    kd_b = batched(lambda h: pair_lanes(kd3, h))
    kdt = jnp.stack([kd_b[g].T for g in range(nb)], axis=0)
    kv = _bdot(kdt, v_b)
    qd_b = batched(lambda h: pair_lanes(qd3, h)).astype(BF16)

    order = range(nc - 1, -1, -1) if rev else range(nc)
    s_in = [None] * nb
    for h in range(RET_HEADS):
        gch = gch_ref[0:1, h * RET_V:(h + 1) * RET_V]
        s_h = s_s[h * V7X_LANES:(h + 1) * V7X_LANES, :]
        for n in order:
            s_in[h * nc + n] = s_h
            s_h = s_h * gch + kv[h * nc + n]
        s_s[h * V7X_LANES:(h + 1) * V7X_LANES, :] = s_h
    o = o_intra + _bdot(qd_b, jnp.stack(s_in, axis=0))
    for h in range(RET_HEADS):
        o_ref[:, h * RET_V:(h + 1) * RET_V] = o[h * nc:(h + 1) * nc].reshape(tt, RET_V).astype(BF16)
    sf_ref[...] = s_s[...]


def _ret(u, rev, bsz, n, tt, tabs, rope_tabs, s0):
    nt = n // tt
    dm, qsc, ksc, gch = tabs

    def tix(i):
        return (nt - 1 - i) if rev else i

    qk_w = RET_HEADS * RET_QK
    full = lambda shape: pl.BlockSpec(shape, lambda b, i: (0,) * len(shape))
    state = pl.BlockSpec((None, RET_HEADS * V7X_LANES, RET_V), lambda b, i: (b, 0, 0))
    in_specs = [pl.BlockSpec((tt, qk_w), lambda b, i: (b * nt + tix(i), COL_RETQ // qk_w)),
                pl.BlockSpec((tt, qk_w), lambda b, i: (b * nt + tix(i), COL_RETK // qk_w)),
                pl.BlockSpec((tt, BRANCH), lambda b, i: (b * nt + tix(i), COL_RETV // BRANCH))]
    args = [u, u, u]
    if rope_tabs is not None:
        in_specs += [pl.BlockSpec((tt, qk_w), lambda b, i: (tix(i), 0))] * 2
        args += list(rope_tabs)
    in_specs += [full(dm.shape), full(qsc.shape), full(ksc.shape), full(gch.shape), state]
    args += [dm, qsc, ksc, gch, s0]
    return pl.pallas_call(
        functools.partial(_ret_kernel, rev, rope_tabs is not None),
        grid=(bsz, nt),
        in_specs=in_specs,
        out_specs=[pl.BlockSpec((tt, BRANCH), lambda b, i: (b * nt + tix(i), 0)), state],
        out_shape=[jax.ShapeDtypeStruct((bsz * n, BRANCH), BF16),
                   jax.ShapeDtypeStruct((bsz, RET_HEADS * V7X_LANES, RET_V), F32)],
        scratch_shapes=[pltpu.VMEM((RET_HEADS * V7X_LANES, RET_V), F32)],
        compiler_params=_cparams(("arbitrary", "arbitrary")),
        name="ret_bwd" if rev else "ret_fwd",
    )(*args)


def _head_rms(a):
    outs = []
    for h in range(BRANCH // V7X_LANES):
        ah = a[:, h * V7X_LANES:(h + 1) * V7X_LANES]
        outs.append(ah * lax.rsqrt(jnp.mean(ah * ah, axis=-1, keepdims=True) + NORM_EPS))
    return jnp.concatenate(outs, axis=1)


def _merge_kernel(final, ona_ref, lf_ref, lb_ref, gf_ref, gb_ref, rf_ref, rb_ref, z_ref, mg_ref, x_ref,
                  gate_ref, gng_ref, wb_ref, wo_ref, fg_ref, o_ref):
    z = z_ref[...].astype(F32)
    sz = _silu(z)
    ya = ona_ref[...].astype(F32) * sz[:, :BRANCH]
    yb = (lf_ref[...].astype(F32) + lb_ref[...].astype(F32)) * sz[:, BRANCH:2 * BRANCH]
    oc = gf_ref[...].astype(F32) + gb_ref[...].astype(F32)
    yc = _head_rms(oc) * gng_ref[...] * sz[:, 2 * BRANCH:3 * BRANCH]
    od = rf_ref[...].astype(F32) + rb_ref[...].astype(F32)
    yd = _head_rms(od) * sz[:, 3 * BRANCH:]
    acc = None
    for idx, y in enumerate((ya, yb, yc, yd)):
        proj = jnp.dot(y.astype(BF16), wb_ref[idx], preferred_element_type=F32)
        gate = _sigmoid(mg_ref[:, idx * D_MODEL:(idx + 1) * D_MODEL].astype(F32))
        acc = gate * proj if acc is None else acc + gate * proj
    out = jnp.dot(acc.astype(BF16), wo_ref[...], preferred_element_type=F32)
    xn = x_ref[...] + gate_ref[...] * out
    if final:
        ms = jnp.mean(xn * xn, axis=-1, keepdims=True)
        xn = xn * lax.rsqrt(ms + NORM_EPS) * fg_ref[...]
    o_ref[...] = xn


def _merge(final, o_na, lru, gdn, ret, u, x2, gate, gng, wb, wo, fg, tm, tiles_per_mod):
    t = x2.shape[0]
    br = pl.BlockSpec((tm, BRANCH), lambda i: (i, 0))
    full = lambda shape: pl.BlockSpec(shape, lambda i: (0,) * len(shape))
    return pl.pallas_call(
        functools.partial(_merge_kernel, final),
        grid=(t // tm,),
        in_specs=[br] * 7 + [
            pl.BlockSpec((tm, 4 * BRANCH), lambda i: (i, COL_Z // (4 * BRANCH))),
            pl.BlockSpec((tm, N_BRANCH * D_MODEL), lambda i: (i, 0)),
            pl.BlockSpec((tm, D_MODEL), lambda i: (i, 0)),
            pl.BlockSpec((None, 1, D_MODEL), lambda i: (i // tiles_per_mod, 0, 0)),
            full((1, BRANCH)), full((N_BRANCH, BRANCH, D_MODEL)), full((D_MODEL, D_MODEL)),
            full((1, D_MODEL))],
        out_specs=pl.BlockSpec((tm, D_MODEL), lambda i: (i, 0)),
        out_shape=jax.ShapeDtypeStruct((t, D_MODEL), F32),
        compiler_params=_cparams(("arbitrary",)),
        name="merge",
    )(o_na, lru[0], lru[1], gdn[0], gdn[1], ret[0], ret[1], u, u, x2, gate, gng, wb, wo, fg)


def _permute_w_in(w):
    def cols(name):
        o, n = _SRC[name]
        return w[:, o:o + n]
    w_p = jnp.concatenate([cols('merge'), cols('na_z'), cols('lru_z'), cols('gdn_z'), cols('ret_z'),
                           cols('gdn_qkv'), cols('na_q'), cols('na_k'), cols('na_v'), cols('lru_x'),
                           cols('ret_q'), cols('ret_k'), cols('ret_v')], axis=1)
    w_ab = jnp.pad(cols('gdn_ab'), ((0, 0), (0, V7X_LANES - 4 * GDN_HEADS)))
    return w_p.astype(BF16), w_ab.astype(BF16)


def _na_bias_tables(rpb):
    col = np.arange(GRID_W)
    col_start = np.clip(col - NA_WIN_COLS // 2, 0, GRID_W - NA_WIN_COLS)
    in_win = (col[None, :] >= col_start[:, None]) & (col[None, :] < col_start[:, None] + NA_WIN_COLS)
    dc = np.clip(col[None, :] - col[:, None] + NA_WIN_COLS - 1, 0, 2 * NA_WIN_COLS - 2)
    t = rpb.astype(F32)[:, :, dc]
    t = jnp.where(in_win[None, None], t, NEG_INF)
    idx = np.arange(NA_WIN_ROWS)[:, None] + np.arange(NA_WIN_ROWS)[None, :]
    tv = t[:, idx]
    return jnp.transpose(tv, (1, 0, 3, 2, 4)).reshape(NA_WIN_ROWS, NA_HEADS, GRID_W, NA_WIN_ROWS * GRID_W)


def _na_group_bias(tvar):
    centre = tvar[NA_WIN_ROWS - 1 - NA_WIN_ROWS // 2]
    slabs = []
    for m in range(NA_GROUP_ROWS):
        lo = 2 * m * GRID_W
        hi = NA_GROUP_KEYS - lo - NA_WIN_ROWS * GRID_W
        slabs.append(jnp.pad(centre, ((0, 0), (0, 0), (lo, hi)), constant_values=NEG_INF))
    return jnp.concatenate(slabs, axis=1)


def _lru_gate_weights(wa, wx):
    def chunk(w, c):
        z = jnp.zeros((LRU_BLOCK, LRU_BLOCK), w.dtype)
        return jnp.concatenate([jnp.concatenate([w[2 * c], z], axis=1),
                                jnp.concatenate([z, w[2 * c + 1]], axis=1)], axis=0)
    return jnp.stack([jnp.concatenate([chunk(wa, c), chunk(wx, c)], axis=1)
                      for c in range(LRU_BLOCKS // 2)], axis=0).astype(BF16)


def _rope_tables(n):
    t = np.arange(n)
    row = (t // GRID_W).astype(np.float32)
    col = (t % GRID_W).astype(np.float32)
    quarter = RET_QK // 4
    inv = jnp.asarray(ROPE_BASE, F32) ** (-jnp.arange(quarter, dtype=F32) / quarter)
    ang_r = jnp.asarray(row)[:, None] * inv[None]
    ang_c = jnp.asarray(col)[:, None] * inv[None]
    ang = jnp.concatenate([ang_r, ang_r, ang_c, ang_c], axis=-1)
    cos, sin = jnp.cos(ang), jnp.sin(ang)
    sign = np.where((np.arange(RET_QK) % 32) < 16, -1.0, 1.0).astype(np.float32)
    return jnp.tile(cos, (1, RET_HEADS)), jnp.tile(sin * sign[None], (1, RET_HEADS))


def _ret_tables(rev):
    log_gamma = jnp.log1p(-jnp.exp2(-(5.0 + jnp.arange(RET_HEADS, dtype=F32))))
    pos = jnp.arange(RET_CHUNK, dtype=F32)
    if rev:
        pos = pos[::-1]
    rel = pos[:, None] - pos[None, :]
    dm = jnp.where(rel >= 0, jnp.exp(jnp.maximum(rel, 0.0)[None] * log_gamma[:, None, None]), 0.0)
    ksc = jnp.exp((RET_CHUNK - 1 - pos)[None] * log_gamma[:, None])
    qsc = jnp.exp((pos + 1.0)[None] * log_gamma[:, None])
    widen = lambda a: jnp.repeat(a.T, RET_QK, axis=1)
    gch = jnp.broadcast_to(jnp.repeat(jnp.exp(RET_CHUNK * log_gamma), RET_V)[None],
                           (V7X_SUBLANES, RET_HEADS * RET_V))
    return dm, widen(qsc), widen(ksc), gch


def _gdn_gate_cols(rev):
    return (GDN_HEADS if rev else 0), (3 * GDN_HEADS if rev else 2 * GDN_HEADS)


def _gdn_gate_row(vals, rev):
    _, aoff = _gdn_gate_cols(rev)
    return jnp.zeros((1, V7X_LANES), F32).at[0, aoff:aoff + GDN_HEADS].set(vals.astype(F32))


def _layer(x2, c2, bsz, n, n_ctx, mod, p, consts, with_ctx_out, final):
    shift_x = mod[:bsz, None, :D_MODEL]
    scale_x = mod[:bsz, None, D_MODEL:2 * D_MODEL]
    gate_x = mod[:bsz, None, 2 * D_MODEL:]
    shift_c = mod[bsz:bsz + 1, None, :D_MODEL]
    scale_c = mod[bsz:bsz + 1, None, D_MODEL:2 * D_MODEL]
    gate_c = mod[bsz:bsz + 1, None, 2 * D_MODEL:]

    tm = 1024 if n % 1024 == 0 else 512
    tmc = min(1024, bsz * n_ctx)
    tt = 512
    tt_ret = 1024 if n % 1024 == 0 else 512
    big = 1 << 30
    u, ab = _inproj(x2, p['norm_g'], shift_x, scale_x, p['w_p'], p['w_ab'], tm, n // tm)
    uc, abc = _inproj(c2, p['norm_g'], shift_c, scale_c, p['w_p'], p['w_ab'], tmc, big)

    o_na = _na_latent(u, uc, p['na_tab'], p['na_grp'], bsz, n, n_ctx)
    gdn_x = _gdn_prep(u, bsz, n, tt, p['gdn_cw'])
    gdn_xc = _gdn_prep(uc, bsz, n_ctx, n_ctx, p['gdn_cw'])

    lru_o, lru_oc = [], []
    gdn_o, gdn_oc = [], []
    ret_o, ret_oc = [], []
    for d, rev in enumerate((False, True)):
        h0 = jnp.zeros((bsz, 1, BRANCH), F32)
        lru_args = (p['lru_cw'], p['lru_cb'], p['lru_wg'][d], p['lru_ba'][d], p['lru_bx'][d], p['lru_lam'][d])
        oc_, hc = _lru(uc, rev, bsz, n_ctx, n_ctx, *lru_args, h0)
        ol_, _ = _lru(u, rev, bsz, n, tt, *lru_args, hc)
        lru_o.append(ol_)
        lru_oc.append(oc_)

        s0 = jnp.zeros((bsz, GDN_HEADS * GDN_DH, GDN_DH), F32)
        gdn_args = (p['gdn_alog'][d], p['gdn_dtb'][d])
        oc_, sc = _gdn(gdn_xc, abc, rev, bsz, n_ctx, n_ctx, *gdn_args, s0)
        ol_, _ = _gdn(gdn_x, ab, rev, bsz, n, tt, *gdn_args, sc)
        gdn_o.append(ol_)
        gdn_oc.append(oc_)

        r0 = jnp.zeros((bsz, RET_HEADS * V7X_LANES, RET_V), F32)
        oc_, rc = _ret(uc, rev, bsz, n_ctx, n_ctx, consts['ret_tabs'][d], None, r0)
        ol_, _ = _ret(u, rev, bsz, n, tt_ret, consts['ret_tabs'][d], consts['rope'], rc)
        ret_o.append(ol_)
        ret_oc.append(oc_)

    merge_w = (p['gdn_ng'], p['w_branch'], p['w_out'], consts['final_g'])
    x_new = _merge(final, o_na, lru_o, gdn_o, ret_o, u, x2, gate_x, *merge_w, 512, n // 512)
    c_new = None
    if with_ctx_out:
        o_nac = _na_ctx(uc, bsz, n_ctx)
        c_new = _merge(False, o_nac, lru_oc, gdn_oc, ret_oc, uc, c2, gate_c, *merge_w, n_ctx, big)
    return x_new, c_new


def kernel(x, c, ctx, c_ctx, norm_g, w_mod, b_mod, w_in, na_rpb, lru_conv_w, lru_conv_b, lru_wa, lru_ba,
           lru_wx, lru_bx, lru_lam, gdn_conv_w, gdn_a_log, gdn_dt_bias, gdn_norm_g, w_branch, w_out,
           final_norm_g):
    bsz, n, d = x.shape
    n_ctx = ctx.shape[1]
    depth = w_in.shape[0]
    assert d == D_MODEL and w_in.shape[2] == D_IN
    assert n % 512 == 0 and n_ctx % max(CHUNK, RET_CHUNK) == 0 and (bsz * n_ctx) % 8 == 0

    rows = -(-(bsz + 1) // V7X_SUBLANES) * V7X_SUBLANES
    cc = jnp.zeros((rows, D_MODEL), F32).at[:bsz].set(c).at[bsz].set(c_ctx)
    mods = _modulation(cc, w_mod, b_mod)

    consts = {
        'rope': _rope_tables(n),
        'ret_tabs': (_ret_tables(False), _ret_tables(True)),
        'final_g': final_norm_g.reshape(1, D_MODEL),
    }

    x2 = x.reshape(bsz * n, D_MODEL)
    c2 = ctx.reshape(bsz * n_ctx, D_MODEL)
    for layer in range(depth):
        w_p, w_ab = _permute_w_in(w_in[layer])
        na_tab = _na_bias_tables(na_rpb[layer])
        p = {
            'na_grp': _na_group_bias(na_tab),
            'norm_g': norm_g[layer].reshape(1, D_MODEL),
            'w_p': w_p, 'w_ab': w_ab,
            'na_tab': na_tab,
            'lru_cw': lru_conv_w[layer], 'lru_cb': lru_conv_b[layer].reshape(1, BRANCH),
            'lru_wg': [_lru_gate_weights(lru_wa[layer, dd], lru_wx[layer, dd]) for dd in range(2)],
            'lru_ba': [lru_ba[layer, dd].reshape(1, BRANCH) for dd in range(2)],
            'lru_bx': [lru_bx[layer, dd].reshape(1, BRANCH) for dd in range(2)],
            'lru_lam': [lru_lam[layer, dd].reshape(1, BRANCH) for dd in range(2)],
            'gdn_cw': gdn_conv_w[layer],
            'gdn_alog': [_gdn_gate_row(gdn_a_log[layer, dd], dd == 1) for dd in range(2)],
            'gdn_dtb': [_gdn_gate_row(gdn_dt_bias[layer, dd], dd == 1) for dd in range(2)],
            'gdn_ng': jnp.tile(gdn_norm_g[layer].astype(F32), GDN_HEADS)[None],
            'w_branch': w_branch[layer].astype(BF16),
            'w_out': w_out[layer].astype(BF16),
        }
        last = layer == depth - 1
        x2, c2 = _layer(x2, c2, bsz, n, n_ctx, mods[layer], p, consts, not last, last)
    return x2.reshape(bsz, n, D_MODEL)
```

```python
import functools
import math

import numpy as np
import jax
import jax.numpy as jnp
from jax import lax
from jax.experimental import pallas as pl
from jax.experimental.pallas import tpu as pltpu

F32 = jnp.float32
BF16 = jnp.bfloat16

D_MODEL = 1024
GRID_W = 64
BRANCH = 512
N_BRANCH = 4
NA_HEADS = 8
NA_DH = 64
NA_WIN_ROWS = 8
NA_WIN_COLS = 16
LRU_BLOCKS = 8
LRU_BLOCK = 64
LRU_C = 8.0
CONV_W = 4
GDN_HEADS = 4
GDN_DH = 128
RET_HEADS = 4
RET_QK = 64
RET_V = 128
CHUNK = 64
LOG2_CHUNK = 6
assert 1 << LOG2_CHUNK == CHUNK
RET_CHUNK = 128
ROPE_BASE = 10000.0
NORM_EPS = 1e-6
L2_EPS = 1e-6
NEG_INF = -1e30

V7X_LANES = 128
V7X_SUBLANES = 8
VMEM_LIMIT = 56 * 1024 * 1024

COL_MERGE = 0
COL_Z = 4096
COL_GDN = 6144
COL_NAQ = 7680
COL_NAK = 8192
COL_NAV = 8704
COL_LRU = 9216
COL_RETQ = 9728
COL_RETK = 9984
COL_RETV = 10240
U_COLS = 10752
IN_TN = 3584

_SRC = {}
_off = 0
for _name, _w in (('na_q', 512), ('na_k', 512), ('na_v', 512), ('na_z', 512), ('lru_x', 512),
                  ('lru_z', 512), ('gdn_qkv', 1536), ('gdn_ab', 16), ('gdn_z', 512), ('ret_q', 256),
                  ('ret_k', 256), ('ret_v', 512), ('ret_z', 512), ('merge', 4096)):
    _SRC[_name] = (_off, _w)
    _off += _w
D_IN = _off


def _cparams(sem):
    return pltpu.CompilerParams(dimension_semantics=sem, vmem_limit_bytes=VMEM_LIMIT)


def _softplus(x):
    return jnp.maximum(x, 0.0) + jnp.log1p(jnp.exp(-jnp.abs(x)))


def _sigmoid(x):
    return 1.0 / (1.0 + jnp.exp(-x))


def _silu(x):
    return x * _sigmoid(x)


def _mod_kernel(c_ref, w_ref, b_ref, o_ref):
    c = c_ref[...]
    a = _silu(c).astype(BF16)
    o_ref[...] = jnp.dot(a, w_ref[...].astype(BF16), preferred_element_type=F32) + b_ref[...]


def _modulation(cc, w_mod, b_mod):
    depth = w_mod.shape[0]
    rows = cc.shape[0]
    tn = 512
    return pl.pallas_call(
        _mod_kernel,
        grid=(depth, 3 * D_MODEL // tn),
        in_specs=[pl.BlockSpec((rows, D_MODEL), lambda l, j: (0, 0)),
                  pl.BlockSpec((None, D_MODEL, tn), lambda l, j: (l, 0, j)),
                  pl.BlockSpec((None, 1, tn), lambda l, j: (l, 0, j))],
        out_specs=pl.BlockSpec((None, rows, tn), lambda l, j: (l, 0, j)),
        out_shape=jax.ShapeDtypeStruct((depth, rows, 3 * D_MODEL), F32),
        compiler_params=_cparams(("arbitrary", "arbitrary")),
        name="modulation",
    )(cc, w_mod, b_mod.reshape(depth, 1, 3 * D_MODEL))


def _inproj_kernel(x_ref, g_ref, sh_ref, sc_ref, w_ref, wab_ref, u_ref, ab_ref, xn_ref):
    @pl.when(pl.program_id(1) == 0)
    def _():
        x = x_ref[...]
        ms = jnp.mean(x * x, axis=-1, keepdims=True)
        xn = x * lax.rsqrt(ms + NORM_EPS) * g_ref[...]
        xn = xn * (1.0 + sc_ref[...]) + sh_ref[...]
        xb = xn.astype(BF16)
        xn_ref[...] = xb
        ab_ref[...] = jnp.dot(xb, wab_ref[...], preferred_element_type=F32)

    u_ref[...] = jnp.dot(xn_ref[...], w_ref[...], preferred_element_type=F32).astype(BF16)


def _inproj(x2, g, shift, scale, w_p, w_ab, tm, tiles_per_mod):
    t = x2.shape[0]
    return pl.pallas_call(
        _inproj_kernel,
        grid=(t // tm, U_COLS // IN_TN),
        in_specs=[pl.BlockSpec((tm, D_MODEL), lambda i, j: (i, 0)),
                  pl.BlockSpec((1, D_MODEL), lambda i, j: (0, 0)),
                  pl.BlockSpec((None, 1, D_MODEL), lambda i, j: (i // tiles_per_mod, 0, 0)),
                  pl.BlockSpec((None, 1, D_MODEL), lambda i, j: (i // tiles_per_mod, 0, 0)),
                  pl.BlockSpec((D_MODEL, IN_TN), lambda i, j: (0, j)),
                  pl.BlockSpec((D_MODEL, V7X_LANES), lambda i, j: (0, 0))],
        out_specs=[pl.BlockSpec((tm, IN_TN), lambda i, j: (i, j)),
                   pl.BlockSpec((tm, V7X_LANES), lambda i, j: (i, 0))],
        out_shape=[jax.ShapeDtypeStruct((t, U_COLS), BF16),
                   jax.ShapeDtypeStruct((t, V7X_LANES), F32)],
        scratch_shapes=[pltpu.VMEM((tm, D_MODEL), BF16)],
        compiler_params=_cparams(("arbitrary", "arbitrary")),
        name="inproj",
    )(x2, g, shift, scale, w_p, w_ab)


NA_GROUP_ROWS = 4
NA_GROUP_KEYS = (2 * (NA_GROUP_ROWS - 1) + NA_WIN_ROWS) * GRID_W


def _na_softmax_pv(qm, kw, vw, kc, vc, bias):
    s_lat = lax.dot_general(qm, kw, (((1,), (1,)), ((), ())), preferred_element_type=F32) + bias
    s_ctx = lax.dot_general(qm, kc, (((1,), (1,)), ((), ())), preferred_element_type=F32)
    m = jnp.maximum(jnp.max(s_lat, axis=-1, keepdims=True), jnp.max(s_ctx, axis=-1, keepdims=True))
    e_lat = jnp.exp(s_lat - m)
    e_ctx = jnp.exp(s_ctx - m)
    den = jnp.sum(e_lat, axis=-1, keepdims=True) + jnp.sum(e_ctx, axis=-1, keepdims=True)
    o = jnp.dot(e_lat.astype(BF16), vw, preferred_element_type=F32)
    o = o + jnp.dot(e_ctx.astype(BF16), vc, preferred_element_type=F32)
    return o / den


def _na_block_case(i, nblk):
    return jnp.where(i == 0, 0, jnp.where(i == nblk - 1, 2, 1))


def _na_kernel(nblk, q_ref, k_ref, v_ref, kc_ref, vc_ref, gb_ref, o_ref):
    i = pl.program_id(1)
    gq = NA_GROUP_ROWS * GRID_W
    lane = lax.broadcasted_iota(jnp.int32, (gq, V7X_LANES), 1)
    lo_half = lane < NA_DH
    for par in range(2):
        koff = jnp.where(i == 0, 0, jnp.where(i == nblk - 1, 2 * GRID_W, par * GRID_W))
        koff = pl.multiple_of(koff, GRID_W)
        for p in range(NA_HEADS // 2):
            ls = slice(p * V7X_LANES, (p + 1) * V7X_LANES)
            kc = kc_ref[:, ls]
            vc = vc_ref[:, ls]
            q_g = jnp.concatenate(
                [q_ref[(2 * m + par) * GRID_W:(2 * m + par + 1) * GRID_W, ls] for m in range(NA_GROUP_ROWS)], axis=0)
            q_g = (q_g.astype(F32) * (NA_DH ** -0.5)).astype(BF16)
            kw = k_ref[pl.ds(koff, NA_GROUP_KEYS), ls]
            vw = v_ref[pl.ds(koff, NA_GROUP_KEYS), ls]
            outs = []
            for e in range(2):
                sel = lo_half if e == 0 else jnp.logical_not(lo_half)
                qm = jnp.where(sel, q_g, jnp.zeros_like(q_g))
                outs.append(_na_softmax_pv(qm, kw, vw, kc, vc, gb_ref[par, 2 * p + e]))
            o_pair = jnp.where(lo_half, outs[0], outs[1]).astype(BF16)
            for m in range(NA_GROUP_ROWS):
                o_ref[(2 * m + par) * GRID_W:(2 * m + par + 1) * GRID_W, ls] = o_pair[m * GRID_W:(m + 1) * GRID_W]


def _na_latent(u, uc, gbias, bsz, n, n_ctx):
    rows = n // GRID_W
    assert rows >= 16 and rows % 8 == 0
    nblk = rows // 8
    tq = 8 * GRID_W
    win = 16 * GRID_W
    u3 = u.reshape(bsz, n, U_COLS)

    def kv_spec(col):
        return pl.BlockSpec((None, pl.Element(win), pl.Element(BRANCH)),
                            lambda b, i: (b, jnp.clip(8 * i - 4, 0, rows - 16) * GRID_W, col))

    return pl.pallas_call(
        functools.partial(_na_kernel, nblk),
        grid=(bsz, nblk),
        in_specs=[pl.BlockSpec((tq, BRANCH), lambda b, i: (b * nblk + i, COL_NAQ // BRANCH)),
                  kv_spec(COL_NAK), kv_spec(COL_NAV),
                  pl.BlockSpec((n_ctx, BRANCH), lambda b, i: (b, COL_NAK // BRANCH)),
                  pl.BlockSpec((n_ctx, BRANCH), lambda b, i: (b, COL_NAV // BRANCH)),
                  pl.BlockSpec((None,) + gbias.shape[1:], lambda b, i: (_na_block_case(i, nblk), 0, 0, 0, 0))],
        out_specs=pl.BlockSpec((tq, BRANCH), lambda b, i: (b * nblk + i, 0)),
        out_shape=jax.ShapeDtypeStruct((bsz * n, BRANCH), BF16),
        compiler_params=_cparams(("arbitrary", "arbitrary")),
        name="na_latent",
    )(u, u3, u3, uc, uc, gbias)


def _na_ctx_kernel(q_ref, k_ref, v_ref, o_ref):
    n_ctx = q_ref.shape[0]
    lane = lax.broadcasted_iota(jnp.int32, (n_ctx, V7X_LANES), 1)
    lo_half = lane < NA_DH
    for p in range(NA_HEADS // 2):
        ls = slice(p * V7X_LANES, (p + 1) * V7X_LANES)
        q_pair = (q_ref[:, ls].astype(F32) * (NA_DH ** -0.5)).astype(BF16)
        k = k_ref[:, ls]
        v = v_ref[:, ls]
        outs = []
        for e in range(2):
            sel = lo_half if e == 0 else jnp.logical_not(lo_half)
            qm = jnp.where(sel, q_pair, jnp.zeros_like(q_pair))
            s = lax.dot_general(qm, k, (((1,), (1,)), ((), ())), preferred_element_type=F32)
            m = jnp.max(s, axis=-1, keepdims=True)
            ex = jnp.exp(s - m)
            den = jnp.sum(ex, axis=-1, keepdims=True)
            outs.append(jnp.dot(ex.astype(BF16), v, preferred_element_type=F32) / den)
        o_ref[:, ls] = jnp.where(lo_half, outs[0], outs[1]).astype(BF16)


def _na_ctx(uc, bsz, n_ctx):
    def spec(col):
        return pl.BlockSpec((n_ctx, BRANCH), lambda b: (b, col // BRANCH))
    return pl.pallas_call(
        _na_ctx_kernel,
        grid=(bsz,),
        in_specs=[spec(COL_NAQ), spec(COL_NAK), spec(COL_NAV)],
        out_specs=pl.BlockSpec((n_ctx, BRANCH), lambda b: (b, 0)),
        out_shape=jax.ShapeDtypeStruct((bsz * n_ctx, BRANCH), BF16),
        compiler_params=_cparams(("arbitrary",)),
        name="na_ctx",
    )(uc, uc, uc)


def _dwconv(x, prev8, next8, w):
    tt = x.shape[0]
    w0, w1, w2, w3 = (w[j:j + 1] for j in range(CONV_W))

    def taps(a):
        n = a.shape[0]
        return (w0 * pltpu.roll(a, 2, 0) + w1 * pltpu.roll(a, 1, 0) + w2 * a
                + w3 * pltpu.roll(a, n - 1, 0))

    y = taps(x)
    head = taps(jnp.concatenate([prev8, x[0:16]], axis=0))[8:16]
    tail = taps(jnp.concatenate([x[tt - 16:tt], next8], axis=0))[8:16]
    return jnp.concatenate([head, y[8:tt - 8], tail], axis=0)


def _tile_index(rev):
    i = pl.program_id(1)
    nt = pl.num_programs(1)
    ti = (nt - 1 - i) if rev else i
    return i, ti, nt


def _scan_specs(tt, nt, bsz, rev, col, width):
    r8 = tt // V7X_SUBLANES
    cb = col // width
    last8 = bsz * nt * r8 - 1

    def tix(i):
        return (nt - 1 - i) if rev else i

    main = pl.BlockSpec((tt, width), lambda b, i: (b * nt + tix(i), cb))
    prev = pl.BlockSpec((V7X_SUBLANES, width),
                        lambda b, i: (jnp.maximum((b * nt + tix(i)) * r8 - 1, 0), cb))
    nxt = pl.BlockSpec((V7X_SUBLANES, width),
                       lambda b, i: (jnp.minimum((b * nt + tix(i) + 1) * r8, last8), cb))
    return main, prev, nxt


def _halo(ref, is_edge):
    h = ref[...].astype(F32)
    return jnp.where(is_edge, jnp.zeros_like(h), h)


def _lru_kernel(rev, x_ref, xp_ref, xn_ref, cw_ref, cb_ref, wg_ref, ba_ref, bx_ref, lam_ref, h0_ref,
                o_ref, hf_ref, a_s, b_s, h_s):
    i, ti, nt = _tile_index(rev)
    tt = x_ref.shape[0]
    ng = tt // V7X_SUBLANES

    @pl.when(i == 0)
    def _():
        h_s[...] = h0_ref[...]

    xs = _dwconv(x_ref[...].astype(F32), _halo(xp_ref, ti == 0), _halo(xn_ref, ti == nt - 1),
                 cw_ref[...]) + cb_ref[...]
    xb = xs.astype(BF16)
    pa, px = [], []
    for c in range(BRANCH // V7X_LANES):
        g = jnp.dot(xb[:, c * V7X_LANES:(c + 1) * V7X_LANES], wg_ref[c], preferred_element_type=F32)
        pa.append(g[:, :V7X_LANES])
        px.append(g[:, V7X_LANES:])
    r_gate = _sigmoid(jnp.concatenate(pa, axis=1) + ba_ref[...])
    i_gate = _sigmoid(jnp.concatenate(px, axis=1) + bx_ref[...])
    z = LRU_C * r_gate * _softplus(-lam_ref[...])
    a = jnp.exp(-z)
    th = jnp.tanh(z)
    b = jnp.sqrt(2.0 * th / (1.0 + th)) * (i_gate * xs)

    row = lax.broadcasted_iota(jnp.int32, (tt, BRANCH), 0) & (V7X_SUBLANES - 1)
    for s in (1, 2, 4):
        if rev:
            a_sh, b_sh, valid = pltpu.roll(a, tt - s, 0), pltpu.roll(b, tt - s, 0), row < V7X_SUBLANES - s
        else:
            a_sh, b_sh, valid = pltpu.roll(a, s, 0), pltpu.roll(b, s, 0), row >= s
        b = jnp.where(valid, b + a * b_sh, b)
        a = jnp.where(valid, a * a_sh, a)
    a_s[...] = a
    b_s[...] = b

    def body(j, h):
        jj = (ng - 1 - j) if rev else j
        r0 = pl.multiple_of(jj * V7X_SUBLANES, V7X_SUBLANES)
        hb = a_s[pl.ds(r0, V7X_SUBLANES), :] * h + b_s[pl.ds(r0, V7X_SUBLANES), :]
        b_s[pl.ds(r0, V7X_SUBLANES), :] = hb
        return hb[0:1] if rev else hb[V7X_SUBLANES - 1:V7X_SUBLANES]

    h = lax.fori_loop(0, ng, body, h_s[...], unroll=4)
    h_s[...] = h
    hf_ref[...] = h
    o_ref[...] = b_s[...].astype(BF16)


def _lru(u, rev, bsz, n, tt, cw, cb, wg, ba, bx, lam, h0):
    nt = n // tt
    main, prev, nxt = _scan_specs(tt, nt, bsz, rev, COL_LRU, BRANCH)

    def tix(i):
        return (nt - 1 - i) if rev else i

    full = lambda shape: pl.BlockSpec(shape, lambda b, i: (0,) * len(shape))
    return pl.pallas_call(
        functools.partial(_lru_kernel, rev),
        grid=(bsz, nt),
        in_specs=[main, prev, nxt, full((CONV_W, BRANCH)), full((1, BRANCH)),
                  full((BRANCH // V7X_LANES, V7X_LANES, 2 * V7X_LANES)),
                  full((1, BRANCH)), full((1, BRANCH)), full((1, BRANCH)),
                  pl.BlockSpec((None, 1, BRANCH), lambda b, i: (b, 0, 0))],
        out_specs=[pl.BlockSpec((tt, BRANCH), lambda b, i: (b * nt + tix(i), 0)),
                   pl.BlockSpec((None, 1, BRANCH), lambda b, i: (b, 0, 0))],
        out_shape=[jax.ShapeDtypeStruct((bsz * n, BRANCH), BF16),
                   jax.ShapeDtypeStruct((bsz, 1, BRANCH), F32)],
        scratch_shapes=[pltpu.VMEM((tt, BRANCH), F32), pltpu.VMEM((tt, BRANCH), F32),
                        pltpu.VMEM((1, BRANCH), F32)],
        compiler_params=_cparams(("arbitrary", "arbitrary")),
        name="lru_bwd" if rev else "lru_fwd",
    )(u, u, u, cw, cb, wg, ba, bx, lam, h0)


def _heads_l2norm(a):
    outs = []
    for h in range(GDN_HEADS):
        ah = a[:, h * GDN_DH:(h + 1) * GDN_DH]
        outs.append(ah * lax.rsqrt(jnp.sum(ah * ah, axis=-1, keepdims=True) + L2_EPS))
    return jnp.concatenate(outs, axis=1)


def _bdot(a, b):
    return jnp.einsum('ncs,nst->nct', a.astype(BF16), b.astype(BF16), preferred_element_type=F32)


def _gdn_prep_kernel(nt, x_ref, xp_ref, xn_ref, cw_ref, o_ref):
    ti = pl.program_id(0) % nt
    qkv = _dwconv(x_ref[...].astype(F32), _halo(xp_ref, ti == 0), _halo(xn_ref, ti == nt - 1), cw_ref[...])
    qkv = _silu(qkv)
    o_ref[:, :BRANCH] = (_heads_l2norm(qkv[:, :BRANCH]) * (GDN_DH ** -0.5)).astype(BF16)
    o_ref[:, BRANCH:2 * BRANCH] = _heads_l2norm(qkv[:, BRANCH:2 * BRANCH]).astype(BF16)
    o_ref[:, 2 * BRANCH:] = qkv[:, 2 * BRANCH:].astype(BF16)


def _gdn_prep(u, bsz, n, tt, cw):
    nt = n // tt
    r8 = tt // V7X_SUBLANES
    width = 3 * BRANCH
    cb = COL_GDN // width
    last8 = bsz * nt * r8 - 1
    return pl.pallas_call(
        functools.partial(_gdn_prep_kernel, nt),
        grid=(bsz * nt,),
        in_specs=[pl.BlockSpec((tt, width), lambda i: (i, cb)),
                  pl.BlockSpec((V7X_SUBLANES, width), lambda i: (jnp.maximum(i * r8 - 1, 0), cb)),
                  pl.BlockSpec((V7X_SUBLANES, width), lambda i: (jnp.minimum((i + 1) * r8, last8), cb)),
                  pl.BlockSpec((CONV_W, width), lambda i: (0, 0))],
        out_specs=pl.BlockSpec((tt, width), lambda i: (i, 0)),
        out_shape=jax.ShapeDtypeStruct((bsz * n, width), BF16),
        compiler_params=_cparams(("arbitrary",)),
        name="gdn_prep",
    )(u, u, u, cw)


def _gdn_kernel(rev, x_ref, ab_ref, alog_ref, dtb_ref, s0_ref,
                o_ref, sf_ref, kw_s, ku_s, qe_s, gl_s, o_s, s_s):
    i = pl.program_id(1)
    gbt, tt = x_ref.shape[0], x_ref.shape[1]
    rt = gbt * tt
    nc = tt // CHUNK
    nct = gbt * nc
    srows = GDN_HEADS * GDN_DH

    @pl.when(i == 0)
    def _():
        s_s[...] = s0_ref[...].reshape(gbt * srows, GDN_DH)

    x = x_ref[...].reshape(rt, 3 * BRANCH)
    q = x[:, :BRANCH].astype(F32)
    k = x[:, BRANCH:2 * BRANCH].astype(F32)
    v = x[:, 2 * BRANCH:].astype(F32)

    ab = ab_ref[...].reshape(rt, V7X_LANES)
    beta_c = _sigmoid(ab)
    gc = -jnp.exp(alog_ref[...]) * _softplus(ab + dtb_ref[...])
    rowc = lax.broadcasted_iota(jnp.int32, (rt, V7X_LANES), 0) & (CHUNK - 1)
    for s in (1 << b for b in range(LOG2_CHUNK)):
        if rev:
            sh, valid = pltpu.roll(gc, rt - s, 0), rowc < CHUNK - s
        else:
            sh, valid = pltpu.roll(gc, s, 0), rowc >= s
        gc = gc + jnp.where(valid, sh, 0.0)
    gcr_c = gc.reshape(nct, CHUNK, V7X_LANES)
    glr_c = gcr_c[:, 0:1, :] if rev else gcr_c[:, CHUNK - 1:CHUNK, :]
    ekd_c = jnp.exp(glr_c - gcr_c).reshape(rt, V7X_LANES)
    egl_c = jnp.broadcast_to(jnp.exp(glr_c), (nct, V7X_SUBLANES, V7X_LANES)).reshape(nct * V7X_SUBLANES, V7X_LANES)

    boff, aoff = _gdn_gate_cols(rev)

    def spread(a, off):
        return jnp.concatenate(
            [jnp.broadcast_to(a[:, off + h:off + h + 1], (a.shape[0], GDN_DH)) for h in range(GDN_HEADS)], axis=1)

    beta = spread(beta_c, boff)
    gcb = spread(gc, aoff)
    eg = spread(jnp.exp(gc), aoff)
    kd = k * spread(ekd_c, aoff)
    gl_s[...] = spread(egl_c, aoff).reshape(nct, V7X_SUBLANES, BRANCH)
    kb = k * beta
    qd = q * eg
    rhs_v = v * beta
    rhs_k = kb * eg

    ci = lax.broadcasted_iota(jnp.int32, (CHUNK, CHUNK), 0)
    si = lax.broadcasted_iota(jnp.int32, (CHUNK, CHUNK), 1)
    incl = (ci <= si) if rev else (ci >= si)
    strict = (ci < si) if rev else (ci > si)
    eye = (ci == si).astype(F32)
    same_blk = [(ci >> (3 + j)) == (si >> (3 + j)) for j in range(LOG2_CHUNK - 2)]

    nb = GDN_HEADS * nct

    def batched(a, col0=0):
        return jnp.concatenate(
            [a[:, col0 + h * GDN_DH:col0 + (h + 1) * GDN_DH].reshape(nct, CHUNK, GDN_DH) for h in range(GDN_HEADS)],
            axis=0)

    k_b = batched(x, BRANCH)
    q_b = batched(x)
    lhs = jnp.concatenate([batched(kb).astype(BF16), q_b], axis=1)
    kkqk = jnp.einsum('ncd,nsd->ncs', lhs, k_b, preferred_element_type=F32)
    gch = batched(gcb)
    gcol = gch[:, :, :CHUNK]
    grow = jnp.stack([gch[g].T[:CHUNK] for g in range(nb)], axis=0)
    diff = gcol - grow
    decay = jnp.where(incl, jnp.exp(jnp.where(incl, diff, 0.0)), 0.0)
    a_low = jnp.where(strict, kkqk[:, :CHUNK] * decay, 0.0)
    attn = jnp.where(incl, kkqk[:, CHUNK:] * decay, 0.0)
    d_blk = jnp.where(same_blk[0], a_low, 0.0)
    tinv = eye - d_blk
    m = d_blk
    for _ in range(2):
        m = _bdot(m, m)
        tinv = tinv + _bdot(tinv, m)
    for lvl in range(LOG2_CHUNK - 3):
        a_off = jnp.where(jnp.logical_and(same_blk[lvl + 1], jnp.logical_not(same_blk[lvl])), a_low, 0.0)
        tinv = tinv - _bdot(_bdot(tinv, a_off), tinv)
    rhs = jnp.concatenate([batched(rhs_v), batched(rhs_k)], axis=2)
    sol = rhs + _bdot(tinv - eye, rhs)
    kd_b = batched(kd)
    kdt = jnp.stack([kd_b[g].T for g in range(nb)], axis=0)
    m2 = _bdot(jnp.concatenate([kdt, attn], axis=1), sol)
    ku_s[...] = m2[:, :GDN_DH, :GDN_DH].reshape(GDN_HEADS, nct, GDN_DH, GDN_DH)
    kw_s[...] = m2[:, :GDN_DH, GDN_DH:].astype(BF16).reshape(GDN_HEADS, nct, GDN_DH, GDN_DH)
    qe = batched(qd) - m2[:, GDN_DH:, GDN_DH:]
    for h in range(GDN_HEADS):
        ls = slice(h * GDN_DH, (h + 1) * GDN_DH)
        o_s[:, ls] = m2[h * nct:(h + 1) * nct, GDN_DH:, :GDN_DH].reshape(rt, GDN_DH)
        qe_s[:, ls] = qe[h * nct:(h + 1) * nct].reshape(rt, GDN_DH).astype(BF16)

    def chunk_body(j, carry):
        for bb in range(gbt):
            cidx = bb * nc + ((nc - 1 - j) if rev else j)
            r0 = pl.multiple_of(cidx * CHUNK, CHUNK)
            for h in range(GDN_HEADS):
                ls = slice(h * GDN_DH, (h + 1) * GDN_DH)
                ss = slice(bb * srows + h * GDN_DH, bb * srows + (h + 1) * GDN_DH)
                s_h = s_s[ss, :]
                s_b = s_h.astype(BF16)
                o_s[pl.ds(r0, CHUNK), ls] += jnp.dot(qe_s[pl.ds(r0, CHUNK), ls], s_b, preferred_element_type=F32)
                gl = gl_s[cidx][0:1, ls]
                s_s[ss, :] = (s_h * gl - jnp.dot(kw_s[h, cidx], s_b, preferred_element_type=F32)) + ku_s[h, cidx]
        return carry

    lax.fori_loop(0, nc, chunk_body, 0)
    o_ref[...] = o_s[...].astype(BF16).reshape(gbt, tt, BRANCH)
    sf_ref[...] = s_s[...].reshape(gbt, srows, GDN_DH)


def _gdn(xn, ab, rev, bsz, n, tt, alog_c, dtb_c, s0):
    nt = n // tt
    gbt = 2 if bsz % 2 == 0 else 1
    nct = gbt * (tt // CHUNK)
    rt = gbt * tt

    def tix(i):
        return (nt - 1 - i) if rev else i

    full = lambda shape: pl.BlockSpec(shape, lambda b, i: (0,) * len(shape))
    state = pl.BlockSpec((gbt, GDN_HEADS * GDN_DH, GDN_DH), lambda b, i: (b, 0, 0))
    o, sf = pl.pallas_call(
        functools.partial(_gdn_kernel, rev),
        grid=(bsz // gbt, nt),
        in_specs=[pl.BlockSpec((gbt, tt, 3 * BRANCH), lambda b, i: (b, tix(i), 0)),
                  pl.BlockSpec((gbt, tt, V7X_LANES), lambda b, i: (b, tix(i), 0)),
                  full((1, V7X_LANES)), full((1, V7X_LANES)), state],
        out_specs=[pl.BlockSpec((gbt, tt, BRANCH), lambda b, i: (b, tix(i), 0)), state],
        out_shape=[jax.ShapeDtypeStruct((bsz, n, BRANCH), BF16),
                   jax.ShapeDtypeStruct((bsz, GDN_HEADS * GDN_DH, GDN_DH), F32)],
        scratch_shapes=[pltpu.VMEM((GDN_HEADS, nct, GDN_DH, GDN_DH), BF16),
                        pltpu.VMEM((GDN_HEADS, nct, GDN_DH, GDN_DH), F32),
                        pltpu.VMEM((rt, BRANCH), BF16),
                        pltpu.VMEM((nct, V7X_SUBLANES, BRANCH), F32),
                        pltpu.VMEM((rt, BRANCH), F32),
                        pltpu.VMEM((gbt * GDN_HEADS * GDN_DH, GDN_DH), F32)],
        compiler_params=_cparams(("arbitrary", "arbitrary")),
        name="gdn_bwd" if rev else "gdn_fwd",
    )(xn.reshape(bsz, n, 3 * BRANCH), ab.reshape(bsz, n, V7X_LANES), alog_c, dtb_c, s0)
    return o.reshape(bsz * n, BRANCH), sf


def _rope(x, cos, sin_signed):
    lane = lax.broadcasted_iota(jnp.int32, x.shape, 1)
    first = (lane & 31) < 16
    w = x.shape[1]
    rot = jnp.where(first, pltpu.roll(x, w - 16, 1), pltpu.roll(x, 16, 1))
    return x * cos + rot * sin_signed


def _ret_kernel(rev, rope, *refs):
    if rope:
        (q_ref, k_ref, v_ref, cos_ref, sin_ref, dm_ref, qsc_ref, ksc_ref, gch_ref, s0_ref,
         o_ref, sf_ref, s_s) = refs
    else:
        (q_ref, k_ref, v_ref, dm_ref, qsc_ref, ksc_ref, gch_ref, s0_ref, o_ref, sf_ref, s_s) = refs
    i = pl.program_id(1)
    tt = q_ref.shape[0]
    nc = tt // RET_CHUNK

    @pl.when(i == 0)
    def _():
        s_s[...] = s0_ref[...]

    q = q_ref[...].astype(F32)
    k = k_ref[...].astype(F32)
    if rope:
        q = _rope(q, cos_ref[...], sin_ref[...])
        k = _rope(k, cos_ref[...], sin_ref[...])
    k = k * (RET_QK ** -0.5)
    q3 = q.reshape(nc, RET_CHUNK, RET_HEADS * RET_QK)
    k3 = k.reshape(nc, RET_CHUNK, RET_HEADS * RET_QK)
    qd3 = q3 * qsc_ref[...]
    kd3 = k3 * ksc_ref[...]
    v = v_ref[...]
    lane = lax.broadcasted_iota(jnp.int32, (nc, RET_CHUNK, V7X_LANES), 2)
    lo_half = lane < RET_QK

    def pair_lanes(a3, h, keep_other=False):
        pair = a3[:, :, (h // 2) * V7X_LANES:(h // 2 + 1) * V7X_LANES]
        if keep_other:
            return pair
        sel = lo_half if h % 2 == 0 else jnp.logical_not(lo_half)
        return jnp.where(sel, pair, 0.0)

    def batched(fn):
        return jnp.concatenate([fn(h) for h in range(RET_HEADS)], axis=0)

    nb = RET_HEADS * nc
    q_b = batched(lambda h: pair_lanes(q3, h)).astype(BF16)
    k_b = batched(lambda h: pair_lanes(k3, h, keep_other=True)).astype(BF16)
    v_b = batched(lambda h: v[:, h * RET_V:(h + 1) * RET_V].reshape(nc, RET_CHUNK, RET_V))
    dm_b = batched(lambda h: jnp.broadcast_to(dm_ref[h], (nc, RET_CHUNK, RET_CHUNK)))
    scores = jnp.einsum('ncd,nsd->ncs', q_b, k_b, preferred_element_type=F32) * dm_b
    o_intra = _bdot(scores, v_b)
    kd_b = batched(lambda h: pair_lanes(kd3, h))
    kdt = jnp.stack([kd_b[g].T for g in range(nb)], axis=0)
    kv = _bdot(kdt, v_b)
    qd_b = batched(lambda h: pair_lanes(qd3, h)).astype(BF16)

    order = range(nc - 1, -1, -1) if rev else range(nc)
    s_in = [None] * nb
    for h in range(RET_HEADS):
        gch = gch_ref[0:1, h * RET_V:(h + 1) * RET_V]
        s_h = s_s[h * V7X_LANES:(h + 1) * V7X_LANES, :]
        for n in order:
            s_in[h * nc + n] = s_h
            s_h = s_h * gch + kv[h * nc + n]
        s_s[h * V7X_LANES:(h + 1) * V7X_LANES, :] = s_h
    o = o_intra + _bdot(qd_b, jnp.stack(s_in, axis=0))
    for h in range(RET_HEADS):
        o_ref[:, h * RET_V:(h + 1) * RET_V] = o[h * nc:(h + 1) * nc].reshape(tt, RET_V).astype(BF16)
    sf_ref[...] = s_s[...]


def _ret(u, rev, bsz, n, tt, tabs, rope_tabs, s0):
    nt = n // tt
    dm, qsc, ksc, gch = tabs

    def tix(i):
        return (nt - 1 - i) if rev else i

    qk_w = RET_HEADS * RET_QK
    full = lambda shape: pl.BlockSpec(shape, lambda b, i: (0,) * len(shape))
    state = pl.BlockSpec((None, RET_HEADS * V7X_LANES, RET_V), lambda b, i: (b, 0, 0))
    in_specs = [pl.BlockSpec((tt, qk_w), lambda b, i: (b * nt + tix(i), COL_RETQ // qk_w)),
                pl.BlockSpec((tt, qk_w), lambda b, i: (b * nt + tix(i), COL_RETK // qk_w)),
                pl.BlockSpec((tt, BRANCH), lambda b, i: (b * nt + tix(i), COL_RETV // BRANCH))]
    args = [u, u, u]
    if rope_tabs is not None:
        in_specs += [pl.BlockSpec((tt, qk_w), lambda b, i: (tix(i), 0))] * 2
        args += list(rope_tabs)
    in_specs += [full(dm.shape), full(qsc.shape), full(ksc.shape), full(gch.shape), state]
    args += [dm, qsc, ksc, gch, s0]
    return pl.pallas_call(
        functools.partial(_ret_kernel, rev, rope_tabs is not None),
        grid=(bsz, nt),
        in_specs=in_specs,
        out_specs=[pl.BlockSpec((tt, BRANCH), lambda b, i: (b * nt + tix(i), 0)), state],
        out_shape=[jax.ShapeDtypeStruct((bsz * n, BRANCH), BF16),
                   jax.ShapeDtypeStruct((bsz, RET_HEADS * V7X_LANES, RET_V), F32)],
        scratch_shapes=[pltpu.VMEM((RET_HEADS * V7X_LANES, RET_V), F32)],
        compiler_params=_cparams(("arbitrary", "arbitrary")),
        name="ret_bwd" if rev else "ret_fwd",
    )(*args)


def _head_rms(a):
    outs = []
    for h in range(BRANCH // V7X_LANES):
        ah = a[:, h * V7X_LANES:(h + 1) * V7X_LANES]
        outs.append(ah * lax.rsqrt(jnp.mean(ah * ah, axis=-1, keepdims=True) + NORM_EPS))
    return jnp.concatenate(outs, axis=1)


def _merge_kernel(final, ona_ref, lf_ref, lb_ref, gf_ref, gb_ref, rf_ref, rb_ref, z_ref, mg_ref, x_ref,
                  gate_ref, gng_ref, wb_ref, wo_ref, fg_ref, o_ref):
    z = z_ref[...].astype(F32)
    sz = _silu(z)
    ya = ona_ref[...].astype(F32) * sz[:, :BRANCH]
    yb = (lf_ref[...].astype(F32) + lb_ref[...].astype(F32)) * sz[:, BRANCH:2 * BRANCH]
    oc = gf_ref[...].astype(F32) + gb_ref[...].astype(F32)
    yc = _head_rms(oc) * gng_ref[...] * sz[:, 2 * BRANCH:3 * BRANCH]
    od = rf_ref[...].astype(F32) + rb_ref[...].astype(F32)
    yd = _head_rms(od) * sz[:, 3 * BRANCH:]
    acc = None
    for idx, y in enumerate((ya, yb, yc, yd)):
        proj = jnp.dot(y.astype(BF16), wb_ref[idx], preferred_element_type=F32)
        gate = _sigmoid(mg_ref[:, idx * D_MODEL:(idx + 1) * D_MODEL].astype(F32))
        acc = gate * proj if acc is None else acc + gate * proj
    out = jnp.dot(acc.astype(BF16), wo_ref[...], preferred_element_type=F32)
    xn = x_ref[...] + gate_ref[...] * out
    if final:
        ms = jnp.mean(xn * xn, axis=-1, keepdims=True)
        xn = xn * lax.rsqrt(ms + NORM_EPS) * fg_ref[...]
    o_ref[...] = xn


def _merge(final, o_na, lru, gdn, ret, u, x2, gate, gng, wb, wo, fg, tm, tiles_per_mod):
    t = x2.shape[0]
    br = pl.BlockSpec((tm, BRANCH), lambda i: (i, 0))
    full = lambda shape: pl.BlockSpec(shape, lambda i: (0,) * len(shape))
    return pl.pallas_call(
        functools.partial(_merge_kernel, final),
        grid=(t // tm,),
        in_specs=[br] * 7 + [
            pl.BlockSpec((tm, 4 * BRANCH), lambda i: (i, COL_Z // (4 * BRANCH))),
            pl.BlockSpec((tm, N_BRANCH * D_MODEL), lambda i: (i, 0)),
            pl.BlockSpec((tm, D_MODEL), lambda i: (i, 0)),
            pl.BlockSpec((None, 1, D_MODEL), lambda i: (i // tiles_per_mod, 0, 0)),
            full((1, BRANCH)), full((N_BRANCH, BRANCH, D_MODEL)), full((D_MODEL, D_MODEL)),
            full((1, D_MODEL))],
        out_specs=pl.BlockSpec((tm, D_MODEL), lambda i: (i, 0)),
        out_shape=jax.ShapeDtypeStruct((t, D_MODEL), F32),
        compiler_params=_cparams(("arbitrary",)),
        name="merge",
    )(o_na, lru[0], lru[1], gdn[0], gdn[1], ret[0], ret[1], u, u, x2, gate, gng, wb, wo, fg)


def _permute_w_in(w):
    def cols(name):
        o, n = _SRC[name]
        return w[:, o:o + n]
    w_p = jnp.concatenate([cols('merge'), cols('na_z'), cols('lru_z'), cols('gdn_z'), cols('ret_z'),
                           cols('gdn_qkv'), cols('na_q'), cols('na_k'), cols('na_v'), cols('lru_x'),
                           cols('ret_q'), cols('ret_k'), cols('ret_v')], axis=1)
    w_ab = jnp.pad(cols('gdn_ab'), ((0, 0), (0, V7X_LANES - 4 * GDN_HEADS)))
    return w_p.astype(BF16), w_ab.astype(BF16)


def _na_bias_tables(rpb):
    col = np.arange(GRID_W)
    col_start = np.clip(col - NA_WIN_COLS // 2, 0, GRID_W - NA_WIN_COLS)
    in_win = (col[None, :] >= col_start[:, None]) & (col[None, :] < col_start[:, None] + NA_WIN_COLS)
    dc = np.clip(col[None, :] - col[:, None] + NA_WIN_COLS - 1, 0, 2 * NA_WIN_COLS - 2)
    onehot = (np.arange(2 * NA_WIN_COLS - 1)[:, None, None] == dc[None]).astype(np.float32)
    t = jnp.einsum('hdj,jqk->hdqk', rpb.astype(F32), onehot, precision=lax.Precision.HIGHEST)
    t = jnp.where(in_win[None, None], t, NEG_INF)
    tv = jnp.stack([t[:, ds:ds + NA_WIN_ROWS] for ds in range(NA_WIN_ROWS)], axis=0)
    return jnp.transpose(tv, (0, 1, 3, 2, 4)).reshape(NA_WIN_ROWS, NA_HEADS, GRID_W, NA_WIN_ROWS * GRID_W)


def _na_group_bias(tvar, rows):
    cases = []
    for first_row, win_start in ((0, (0, 0)), (None, None), (rows - 8, (rows - 14, rows - 14))):
        pars = []
        for par in range(2):
            slabs = []
            for m in range(NA_GROUP_ROWS):
                if first_row is None:
                    off, dstart = 2 * m, NA_WIN_ROWS - 1 - NA_WIN_ROWS // 2
                else:
                    r = first_row + 2 * m + par
                    row_start = min(max(r - NA_WIN_ROWS // 2, 0), rows - NA_WIN_ROWS)
                    off, dstart = row_start - win_start[par], row_start - r + NA_WIN_ROWS - 1
                lo = off * GRID_W
                hi = NA_GROUP_KEYS - lo - NA_WIN_ROWS * GRID_W
                assert lo >= 0 and hi >= 0
                slabs.append(jnp.pad(tvar[dstart], ((0, 0), (0, 0), (lo, hi)), constant_values=NEG_INF))
            pars.append(jnp.concatenate(slabs, axis=1))
        cases.append(jnp.stack(pars, axis=0))
    return jnp.stack(cases, axis=0)


def _lru_gate_weights(wa, wx):
    def chunk(w, c):
        z = jnp.zeros((LRU_BLOCK, LRU_BLOCK), w.dtype)
        return jnp.concatenate([jnp.concatenate([w[2 * c], z], axis=1),
                                jnp.concatenate([z, w[2 * c + 1]], axis=1)], axis=0)
    return jnp.stack([jnp.concatenate([chunk(wa, c), chunk(wx, c)], axis=1)
                      for c in range(LRU_BLOCKS // 2)], axis=0).astype(BF16)


def _rope_tables(n):
    t = np.arange(n)
    row = (t // GRID_W).astype(np.float32)
    col = (t % GRID_W).astype(np.float32)
    quarter = RET_QK // 4
    inv = jnp.asarray(ROPE_BASE, F32) ** (-jnp.arange(quarter, dtype=F32) / quarter)
    ang_r = jnp.asarray(row)[:, None] * inv[None]
    ang_c = jnp.asarray(col)[:, None] * inv[None]
    ang = jnp.concatenate([ang_r, ang_r, ang_c, ang_c], axis=-1)
    cos, sin = jnp.cos(ang), jnp.sin(ang)
    sign = np.where((np.arange(RET_QK) % 32) < 16, -1.0, 1.0).astype(np.float32)
    return jnp.tile(cos, (1, RET_HEADS)), jnp.tile(sin * sign[None], (1, RET_HEADS))


def _ret_tables(rev):
    log_gamma = jnp.log1p(-jnp.exp2(-(5.0 + jnp.arange(RET_HEADS, dtype=F32))))
    pos = jnp.arange(RET_CHUNK, dtype=F32)
    if rev:
        pos = pos[::-1]
    rel = pos[:, None] - pos[None, :]
    dm = jnp.where(rel >= 0, jnp.exp(jnp.maximum(rel, 0.0)[None] * log_gamma[:, None, None]), 0.0)
    ksc = jnp.exp((RET_CHUNK - 1 - pos)[None] * log_gamma[:, None])
    qsc = jnp.exp((pos + 1.0)[None] * log_gamma[:, None])
    widen = lambda a: jnp.repeat(a.T, RET_QK, axis=1)
    gch = jnp.broadcast_to(jnp.repeat(jnp.exp(RET_CHUNK * log_gamma), RET_V)[None],
                           (V7X_SUBLANES, RET_HEADS * RET_V))
    return dm, widen(qsc), widen(ksc), gch


def _gdn_gate_cols(rev):
    return (GDN_HEADS if rev else 0), (3 * GDN_HEADS if rev else 2 * GDN_HEADS)


def _gdn_gate_row(vals, rev):
    _, aoff = _gdn_gate_cols(rev)
    return jnp.zeros((1, V7X_LANES), F32).at[0, aoff:aoff + GDN_HEADS].set(vals.astype(F32))


def _layer(x2, c2, bsz, n, n_ctx, mod, p, consts, with_ctx_out, final):
    shift_x = mod[:bsz, None, :D_MODEL]
    scale_x = mod[:bsz, None, D_MODEL:2 * D_MODEL]
    gate_x = mod[:bsz, None, 2 * D_MODEL:]
    shift_c = mod[bsz:bsz + 1, None, :D_MODEL]
    scale_c = mod[bsz:bsz + 1, None, D_MODEL:2 * D_MODEL]
    gate_c = mod[bsz:bsz + 1, None, 2 * D_MODEL:]

    tm = 1024 if n % 1024 == 0 else 512
    tmc = min(1024, bsz * n_ctx)
    tt = 512
    tt_ret = 1024 if n % 1024 == 0 else 512
    big = 1 << 30
    u, ab = _inproj(x2, p['norm_g'], shift_x, scale_x, p['w_p'], p['w_ab'], tm, n // tm)
    uc, abc = _inproj(c2, p['norm_g'], shift_c, scale_c, p['w_p'], p['w_ab'], tmc, big)

    o_na = _na_latent(u, uc, p['na_grp'], bsz, n, n_ctx)
    gdn_x = _gdn_prep(u, bsz, n, tt, p['gdn_cw'])
    gdn_xc = _gdn_prep(uc, bsz, n_ctx, n_ctx, p['gdn_cw'])

    lru_o, lru_oc = [], []
    gdn_o, gdn_oc = [], []
    ret_o, ret_oc = [], []
    for d, rev in enumerate((False, True)):
        h0 = jnp.zeros((bsz, 1, BRANCH), F32)
        lru_args = (p['lru_cw'], p['lru_cb'], p['lru_wg'][d], p['lru_ba'][d], p['lru_bx'][d], p['lru_lam'][d])
        oc_, hc = _lru(uc, rev, bsz, n_ctx, n_ctx, *lru_args, h0)
        ol_, _ = _lru(u, rev, bsz, n, tt_ret, *lru_args, hc)
        lru_o.append(ol_)
        lru_oc.append(oc_)

        s0 = jnp.zeros((bsz, GDN_HEADS * GDN_DH, GDN_DH), F32)
        gdn_args = (p['gdn_alog'][d], p['gdn_dtb'][d])
        oc_, sc = _gdn(gdn_xc, abc, rev, bsz, n_ctx, n_ctx, *gdn_args, s0)
        ol_, _ = _gdn(gdn_x, ab, rev, bsz, n, tt, *gdn_args, sc)
        gdn_o.append(ol_)
        gdn_oc.append(oc_)

        r0 = jnp.zeros((bsz, RET_HEADS * V7X_LANES, RET_V), F32)
        oc_, rc = _ret(uc, rev, bsz, n_ctx, n_ctx, consts['ret_tabs'][d], None, r0)
        ol_, _ = _ret(u, rev, bsz, n, tt_ret, consts['ret_tabs'][d], consts['rope'], rc)
        ret_o.append(ol_)
        ret_oc.append(oc_)

    merge_w = (p['gdn_ng'], p['w_branch'], p['w_out'], consts['final_g'])
    x_new = _merge(final, o_na, lru_o, gdn_o, ret_o, u, x2, gate_x, *merge_w, 512, n // 512)
    c_new = None
    if with_ctx_out:
        o_nac = _na_ctx(uc, bsz, n_ctx)
        c_new = _merge(False, o_nac, lru_oc, gdn_oc, ret_oc, uc, c2, gate_c, *merge_w, n_ctx, big)
    return x_new, c_new


def kernel(x, c, ctx, c_ctx, norm_g, w_mod, b_mod, w_in, na_rpb, lru_conv_w, lru_conv_b, lru_wa, lru_ba,
           lru_wx, lru_bx, lru_lam, gdn_conv_w, gdn_a_log, gdn_dt_bias, gdn_norm_g, w_branch, w_out,
           final_norm_g):
    bsz, n, d = x.shape
    n_ctx = ctx.shape[1]
    depth = w_in.shape[0]
    assert d == D_MODEL and w_in.shape[2] == D_IN
    assert n % 512 == 0 and n_ctx % max(CHUNK, RET_CHUNK) == 0 and (bsz * n_ctx) % 8 == 0

    rows = -(-(bsz + 1) // V7X_SUBLANES) * V7X_SUBLANES
    cc = jnp.zeros((rows, D_MODEL), F32).at[:bsz].set(c).at[bsz].set(c_ctx)
    mods = _modulation(cc, w_mod, b_mod)

    consts = {
        'rope': _rope_tables(n),
        'ret_tabs': (_ret_tables(False), _ret_tables(True)),
        'final_g': final_norm_g.reshape(1, D_MODEL),
    }

    x2 = x.reshape(bsz * n, D_MODEL)
    c2 = ctx.reshape(bsz * n_ctx, D_MODEL)
    for layer in range(depth):
        w_p, w_ab = _permute_w_in(w_in[layer])
        p = {
            'na_grp': _na_group_bias(_na_bias_tables(na_rpb[layer]), n // GRID_W),
            'norm_g': norm_g[layer].reshape(1, D_MODEL),
            'w_p': w_p, 'w_ab': w_ab,
            'lru_cw': lru_conv_w[layer], 'lru_cb': lru_conv_b[layer].reshape(1, BRANCH),
            'lru_wg': [_lru_gate_weights(lru_wa[layer, dd], lru_wx[layer, dd]) for dd in range(2)],
            'lru_ba': [lru_ba[layer, dd].reshape(1, BRANCH) for dd in range(2)],
            'lru_bx': [lru_bx[layer, dd].reshape(1, BRANCH) for dd in range(2)],
            'lru_lam': [lru_lam[layer, dd].reshape(1, BRANCH) for dd in range(2)],
            'gdn_cw': gdn_conv_w[layer],
            'gdn_alog': [_gdn_gate_row(gdn_a_log[layer, dd], dd == 1) for dd in range(2)],
            'gdn_dtb': [_gdn_gate_row(gdn_dt_bias[layer, dd], dd == 1) for dd in range(2)],
            'gdn_ng': jnp.tile(gdn_norm_g[layer].astype(F32), GDN_HEADS)[None],
            'w_branch': w_branch[layer].astype(BF16),
            'w_out': w_out[layer].astype(BF16),
        }
        last = layer == depth - 1
        x2, c2 = _layer(x2, c2, bsz, n, n_ctx, mods[layer], p, consts, not last, last)
    return x2.reshape(bsz, n, D_MODEL)
```

```python
import functools
import math

import numpy as np
import jax
import jax.numpy as jnp
from jax import lax
from jax.experimental import pallas as pl
from jax.experimental.pallas import tpu as pltpu

F32 = jnp.float32
BF16 = jnp.bfloat16

D_MODEL = 1024
GRID_W = 64
BRANCH = 512
N_BRANCH = 4
NA_HEADS = 8
NA_DH = 64
NA_WIN_ROWS = 8
NA_WIN_COLS = 16
LRU_BLOCKS = 8
LRU_BLOCK = 64
LRU_C = 8.0
CONV_W = 4
GDN_HEADS = 4
GDN_DH = 128
RET_HEADS = 4
RET_QK = 64
RET_V = 128
CHUNK = 64
LOG2_CHUNK = 6
assert 1 << LOG2_CHUNK == CHUNK
RET_CHUNK = 128
ROPE_BASE = 10000.0
NORM_EPS = 1e-6
L2_EPS = 1e-6
NEG_INF = -1e30

V7X_LANES = 128
V7X_SUBLANES = 8
VMEM_LIMIT = 56 * 1024 * 1024

COL_MERGE = 0
COL_Z = 4096
COL_GDN = 6144
COL_NAQ = 7680
COL_NAK = 8192
COL_NAV = 8704
COL_LRU = 9216
COL_RETQ = 9728
COL_RETK = 9984
COL_RETV = 10240
U_COLS = 10752
IN_TN = 3584

_SRC = {}
_off = 0
for _name, _w in (('na_q', 512), ('na_k', 512), ('na_v', 512), ('na_z', 512), ('lru_x', 512),
                  ('lru_z', 512), ('gdn_qkv', 1536), ('gdn_ab', 16), ('gdn_z', 512), ('ret_q', 256),
                  ('ret_k', 256), ('ret_v', 512), ('ret_z', 512), ('merge', 4096)):
    _SRC[_name] = (_off, _w)
    _off += _w
D_IN = _off


def _cparams(sem):
    return pltpu.CompilerParams(dimension_semantics=sem, vmem_limit_bytes=VMEM_LIMIT)


def _softplus(x):
    return jnp.maximum(x, 0.0) + jnp.log1p(jnp.exp(-jnp.abs(x)))


NEG_LOG2_E = -1.4426950408889634


def _sigmoid(x):
    return 1.0 / (1.0 + jnp.exp2(x * NEG_LOG2_E))


def _silu(x):
    return x * _sigmoid(x)


def _mod_kernel(c_ref, w_ref, b_ref, o_ref):
    c = c_ref[...]
    a = _silu(c).astype(BF16)
    o_ref[...] = jnp.dot(a, w_ref[...].astype(BF16), preferred_element_type=F32) + b_ref[...]


def _modulation(cc, w_mod, b_mod):
    depth = w_mod.shape[0]
    rows = cc.shape[0]
    tn = 512
    return pl.pallas_call(
        _mod_kernel,
        grid=(depth, 3 * D_MODEL // tn),
        in_specs=[pl.BlockSpec((rows, D_MODEL), lambda l, j: (0, 0)),
                  pl.BlockSpec((None, D_MODEL, tn), lambda l, j: (l, 0, j)),
                  pl.BlockSpec((None, 1, tn), lambda l, j: (l, 0, j))],
        out_specs=pl.BlockSpec((None, rows, tn), lambda l, j: (l, 0, j)),
        out_shape=jax.ShapeDtypeStruct((depth, rows, 3 * D_MODEL), F32),
        compiler_params=_cparams(("arbitrary", "arbitrary")),
        name="modulation",
    )(cc, w_mod, b_mod.reshape(depth, 1, 3 * D_MODEL))


def _inproj_kernel(x_ref, g_ref, sh_ref, sc_ref, w_ref, wab_ref, u_ref, ab_ref, xn_ref):
    @pl.when(pl.program_id(1) == 0)
    def _():
        x = x_ref[...]
        ms = jnp.mean(x * x, axis=-1, keepdims=True)
        xn = x * lax.rsqrt(ms + NORM_EPS) * g_ref[...]
        xn = xn * (1.0 + sc_ref[...]) + sh_ref[...]
        xb = xn.astype(BF16)
        xn_ref[...] = xb
        ab_ref[...] = jnp.dot(xb, wab_ref[...], preferred_element_type=F32)

    u_ref[...] = jnp.dot(xn_ref[...], w_ref[...], preferred_element_type=F32).astype(BF16)


def _inproj(x2, g, shift, scale, w_p, w_ab, tm, tiles_per_mod):
    t = x2.shape[0]
    return pl.pallas_call(
        _inproj_kernel,
        grid=(t // tm, U_COLS // IN_TN),
        in_specs=[pl.BlockSpec((tm, D_MODEL), lambda i, j: (i, 0)),
                  pl.BlockSpec((1, D_MODEL), lambda i, j: (0, 0)),
                  pl.BlockSpec((None, 1, D_MODEL), lambda i, j: (i // tiles_per_mod, 0, 0)),
                  pl.BlockSpec((None, 1, D_MODEL), lambda i, j: (i // tiles_per_mod, 0, 0)),
                  pl.BlockSpec((D_MODEL, IN_TN), lambda i, j: (0, j)),
                  pl.BlockSpec((D_MODEL, V7X_LANES), lambda i, j: (0, 0))],
        out_specs=[pl.BlockSpec((tm, IN_TN), lambda i, j: (i, j)),
                   pl.BlockSpec((tm, V7X_LANES), lambda i, j: (i, 0))],
        out_shape=[jax.ShapeDtypeStruct((t, U_COLS), BF16),
                   jax.ShapeDtypeStruct((t, V7X_LANES), F32)],
        scratch_shapes=[pltpu.VMEM((tm, D_MODEL), BF16)],
        compiler_params=_cparams(("arbitrary", "arbitrary")),
        name="inproj",
    )(x2, g, shift, scale, w_p, w_ab)


NA_GROUP_ROWS = 4
NA_GROUP_KEYS = (2 * (NA_GROUP_ROWS - 1) + NA_WIN_ROWS) * GRID_W


def _na_softmax_pv(qm, kw, vw, kc, vc, bias):
    s_lat = lax.dot_general(qm, kw, (((1,), (1,)), ((), ())), preferred_element_type=F32) + bias
    s_ctx = lax.dot_general(qm, kc, (((1,), (1,)), ((), ())), preferred_element_type=F32)
    m = jnp.maximum(jnp.max(s_lat, axis=-1, keepdims=True), jnp.max(s_ctx, axis=-1, keepdims=True))
    e_lat = jnp.exp(s_lat - m)
    e_ctx = jnp.exp(s_ctx - m)
    den = jnp.sum(e_lat, axis=-1, keepdims=True) + jnp.sum(e_ctx, axis=-1, keepdims=True)
    o = jnp.dot(e_lat.astype(BF16), vw, preferred_element_type=F32)
    o = o + jnp.dot(e_ctx.astype(BF16), vc, preferred_element_type=F32)
    return o / den


def _na_row_geometry(case, par, m, rows):
    if case == 1:
        return 2 * m, NA_WIN_ROWS - 1 - NA_WIN_ROWS // 2
    first_row, win_start = (0, 0) if case == 0 else (rows - 8, rows - 14)
    r = first_row + 2 * m + par
    row_start = min(max(r - NA_WIN_ROWS // 2, 0), rows - NA_WIN_ROWS)
    return row_start - win_start, row_start - r + NA_WIN_ROWS - 1


def _na_build_bias(case, rows, t_ref, gb_s):
    def head_body(h, carry):
        for par in range(2):
            for m in range(NA_GROUP_ROWS):
                off, dstart = _na_row_geometry(case, par, m, rows)
                lo = off * GRID_W
                hi = NA_GROUP_KEYS - lo - NA_WIN_ROWS * GRID_W
                assert lo >= 0 and hi >= 0
                parts = [t_ref[dstart, h]]
                if lo:
                    parts.insert(0, jnp.full((GRID_W, lo), NEG_INF, F32))
                if hi:
                    parts.append(jnp.full((GRID_W, hi), NEG_INF, F32))
                gb_s[par, h, m * GRID_W:(m + 1) * GRID_W, :] = jnp.concatenate(parts, axis=1)
        return carry

    lax.fori_loop(0, NA_HEADS, head_body, 0)


def _na_kernel(nblk, rows, q_ref, k_ref, v_ref, kc_ref, vc_ref, t_ref, o_ref, gb_ref):
    i = pl.program_id(1)

    @pl.when(i == 0)
    def _():
        _na_build_bias(0, rows, t_ref, gb_ref)

    @pl.when(i == 1)
    def _():
        _na_build_bias(1, rows, t_ref, gb_ref)

    @pl.when(i == nblk - 1)
    def _():
        _na_build_bias(2, rows, t_ref, gb_ref)

    gq = NA_GROUP_ROWS * GRID_W
    lane = lax.broadcasted_iota(jnp.int32, (gq, V7X_LANES), 1)
    lo_half = lane < NA_DH
    for par in range(2):
        koff = jnp.where(i == 0, 0, jnp.where(i == nblk - 1, 2 * GRID_W, par * GRID_W))
        koff = pl.multiple_of(koff, GRID_W)
        for p in range(NA_HEADS // 2):
            ls = slice(p * V7X_LANES, (p + 1) * V7X_LANES)
            kc = kc_ref[:, ls]
            vc = vc_ref[:, ls]
            q_g = jnp.concatenate(
                [q_ref[(2 * m + par) * GRID_W:(2 * m + par + 1) * GRID_W, ls] for m in range(NA_GROUP_ROWS)], axis=0)
            q_g = (q_g.astype(F32) * (NA_DH ** -0.5)).astype(BF16)
            kw = k_ref[pl.ds(koff, NA_GROUP_KEYS), ls]
            vw = v_ref[pl.ds(koff, NA_GROUP_KEYS), ls]
            outs = []
            for e in range(2):
                sel = lo_half if e == 0 else jnp.logical_not(lo_half)
                qm = jnp.where(sel, q_g, jnp.zeros_like(q_g))
                outs.append(_na_softmax_pv(qm, kw, vw, kc, vc, gb_ref[par, 2 * p + e]))
            o_pair = jnp.where(lo_half, outs[0], outs[1]).astype(BF16)
            for m in range(NA_GROUP_ROWS):
                o_ref[(2 * m + par) * GRID_W:(2 * m + par + 1) * GRID_W, ls] = o_pair[m * GRID_W:(m + 1) * GRID_W]


def _na_latent(u, uc, tvar, bsz, n, n_ctx):
    rows = n // GRID_W
    assert rows >= 16 and rows % 8 == 0
    nblk = rows // 8
    tq = 8 * GRID_W
    win = 16 * GRID_W
    u3 = u.reshape(bsz, n, U_COLS)

    def kv_spec(col):
        return pl.BlockSpec((None, pl.Element(win), pl.Element(BRANCH)),
                            lambda b, i: (b, jnp.clip(8 * i - 4, 0, rows - 16) * GRID_W, col))

    return pl.pallas_call(
        functools.partial(_na_kernel, nblk, rows),
        grid=(bsz, nblk),
        in_specs=[pl.BlockSpec((tq, BRANCH), lambda b, i: (b * nblk + i, COL_NAQ // BRANCH)),
                  kv_spec(COL_NAK), kv_spec(COL_NAV),
                  pl.BlockSpec((n_ctx, BRANCH), lambda b, i: (b, COL_NAK // BRANCH)),
                  pl.BlockSpec((n_ctx, BRANCH), lambda b, i: (b, COL_NAV // BRANCH)),
                  pl.BlockSpec(tvar.shape, lambda b, i: (0, 0, 0, 0), pipeline_mode=pl.Buffered(1))],
        out_specs=pl.BlockSpec((tq, BRANCH), lambda b, i: (b * nblk + i, 0)),
        out_shape=jax.ShapeDtypeStruct((bsz * n, BRANCH), BF16),
        scratch_shapes=[pltpu.VMEM((2, NA_HEADS, NA_GROUP_ROWS * GRID_W, NA_GROUP_KEYS), F32)],
        compiler_params=_cparams(("arbitrary", "arbitrary")),
        name="na_latent",
    )(u, u3, u3, uc, uc, tvar)


def _na_ctx_kernel(q_ref, k_ref, v_ref, o_ref):
    n_ctx = q_ref.shape[0]
    lane = lax.broadcasted_iota(jnp.int32, (n_ctx, V7X_LANES), 1)
    lo_half = lane < NA_DH
    for p in range(NA_HEADS // 2):
        ls = slice(p * V7X_LANES, (p + 1) * V7X_LANES)
        q_pair = (q_ref[:, ls].astype(F32) * (NA_DH ** -0.5)).astype(BF16)
        k = k_ref[:, ls]
        v = v_ref[:, ls]
        outs = []
        for e in range(2):
            sel = lo_half if e == 0 else jnp.logical_not(lo_half)
            qm = jnp.where(sel, q_pair, jnp.zeros_like(q_pair))
            s = lax.dot_general(qm, k, (((1,), (1,)), ((), ())), preferred_element_type=F32)
            m = jnp.max(s, axis=-1, keepdims=True)
            ex = jnp.exp(s - m)
            den = jnp.sum(ex, axis=-1, keepdims=True)
            outs.append(jnp.dot(ex.astype(BF16), v, preferred_element_type=F32) / den)
        o_ref[:, ls] = jnp.where(lo_half, outs[0], outs[1]).astype(BF16)


def _na_ctx(uc, bsz, n_ctx):
    def spec(col):
        return pl.BlockSpec((n_ctx, BRANCH), lambda b: (b, col // BRANCH))
    return pl.pallas_call(
        _na_ctx_kernel,
        grid=(bsz,),
        in_specs=[spec(COL_NAQ), spec(COL_NAK), spec(COL_NAV)],
        out_specs=pl.BlockSpec((n_ctx, BRANCH), lambda b: (b, 0)),
        out_shape=jax.ShapeDtypeStruct((bsz * n_ctx, BRANCH), BF16),
        compiler_params=_cparams(("arbitrary",)),
        name="na_ctx",
    )(uc, uc, uc)


def _dwconv(x, prev8, next8, w):
    tt = x.shape[0]
    w0, w1, w2, w3 = (w[j:j + 1] for j in range(CONV_W))

    def taps(a):
        n = a.shape[0]
        return (w0 * pltpu.roll(a, 2, 0) + w1 * pltpu.roll(a, 1, 0) + w2 * a
                + w3 * pltpu.roll(a, n - 1, 0))

    y = taps(x)
    head = taps(jnp.concatenate([prev8, x[0:16]], axis=0))[8:16]
    tail = taps(jnp.concatenate([x[tt - 16:tt], next8], axis=0))[8:16]
    return jnp.concatenate([head, y[8:tt - 8], tail], axis=0)


def _tile_index(rev):
    i = pl.program_id(1)
    nt = pl.num_programs(1)
    ti = (nt - 1 - i) if rev else i
    return i, ti, nt


def _scan_specs(tt, nt, bsz, rev, col, width):
    r8 = tt // V7X_SUBLANES
    cb = col // width
    last8 = bsz * nt * r8 - 1

    def tix(i):
        return (nt - 1 - i) if rev else i

    main = pl.BlockSpec((tt, width), lambda b, i: (b * nt + tix(i), cb))
    prev = pl.BlockSpec((V7X_SUBLANES, width),
                        lambda b, i: (jnp.maximum((b * nt + tix(i)) * r8 - 1, 0), cb))
    nxt = pl.BlockSpec((V7X_SUBLANES, width),
                       lambda b, i: (jnp.minimum((b * nt + tix(i) + 1) * r8, last8), cb))
    return main, prev, nxt


def _halo(ref, is_edge):
    h = ref[...].astype(F32)
    return jnp.where(is_edge, jnp.zeros_like(h), h)


def _lru_kernel(rev, x_ref, xp_ref, xn_ref, cw_ref, cb_ref, wg_ref, ba_ref, bx_ref, lam_ref, h0_ref,
                o_ref, hf_ref, a_s, b_s, h_s):
    i, ti, nt = _tile_index(rev)
    tt = x_ref.shape[0]
    ng = tt // V7X_SUBLANES

    @pl.when(i == 0)
    def _():
        h_s[...] = h0_ref[...]

    xs = _dwconv(x_ref[...].astype(F32), _halo(xp_ref, ti == 0), _halo(xn_ref, ti == nt - 1),
                 cw_ref[...]) + cb_ref[...]
    xb = xs.astype(BF16)
    pa, px = [], []
    for c in range(BRANCH // V7X_LANES):
        g = jnp.dot(xb[:, c * V7X_LANES:(c + 1) * V7X_LANES], wg_ref[c], preferred_element_type=F32)
        pa.append(g[:, :V7X_LANES])
        px.append(g[:, V7X_LANES:])
    r_gate = _sigmoid(jnp.concatenate(pa, axis=1) + ba_ref[...])
    i_gate = _sigmoid(jnp.concatenate(px, axis=1) + bx_ref[...])
    z = LRU_C * r_gate * _softplus(-lam_ref[...])
    a = jnp.exp(-z)
    th = jnp.tanh(z)
    b = jnp.sqrt(2.0 * th / (1.0 + th)) * (i_gate * xs)

    row = lax.broadcasted_iota(jnp.int32, (tt, BRANCH), 0) & (V7X_SUBLANES - 1)
    for s in (1, 2, 4):
        if rev:
            a_sh, b_sh, valid = pltpu.roll(a, tt - s, 0), pltpu.roll(b, tt - s, 0), row < V7X_SUBLANES - s
        else:
            a_sh, b_sh, valid = pltpu.roll(a, s, 0), pltpu.roll(b, s, 0), row >= s
        b = jnp.where(valid, b + a * b_sh, b)
        a = jnp.where(valid, a * a_sh, a)
    a_s[...] = a
    b_s[...] = b

    def body(j, h):
        jj = (ng - 1 - j) if rev else j
        r0 = pl.multiple_of(jj * V7X_SUBLANES, V7X_SUBLANES)
        hb = a_s[pl.ds(r0, V7X_SUBLANES), :] * h + b_s[pl.ds(r0, V7X_SUBLANES), :]
        b_s[pl.ds(r0, V7X_SUBLANES), :] = hb
        return hb[0:1] if rev else hb[V7X_SUBLANES - 1:V7X_SUBLANES]

    h = lax.fori_loop(0, ng, body, h_s[...], unroll=4)
    h_s[...] = h
    hf_ref[...] = h
    o_ref[...] = b_s[...].astype(BF16)


def _lru(u, rev, bsz, n, tt, cw, cb, wg, ba, bx, lam, h0):
    nt = n // tt
    main, prev, nxt = _scan_specs(tt, nt, bsz, rev, COL_LRU, BRANCH)

    def tix(i):
        return (nt - 1 - i) if rev else i

    full = lambda shape: pl.BlockSpec(shape, lambda b, i: (0,) * len(shape))
    return pl.pallas_call(
        functools.partial(_lru_kernel, rev),
        grid=(bsz, nt),
        in_specs=[main, prev, nxt, full((CONV_W, BRANCH)), full((1, BRANCH)),
                  full((BRANCH // V7X_LANES, V7X_LANES, 2 * V7X_LANES)),
                  full((1, BRANCH)), full((1, BRANCH)), full((1, BRANCH)),
                  pl.BlockSpec((None, 1, BRANCH), lambda b, i: (b, 0, 0))],
        out_specs=[pl.BlockSpec((tt, BRANCH), lambda b, i: (b * nt + tix(i), 0)),
                   pl.BlockSpec((None, 1, BRANCH), lambda b, i: (b, 0, 0))],
        out_shape=[jax.ShapeDtypeStruct((bsz * n, BRANCH), BF16),
                   jax.ShapeDtypeStruct((bsz, 1, BRANCH), F32)],
        scratch_shapes=[pltpu.VMEM((tt, BRANCH), F32), pltpu.VMEM((tt, BRANCH), F32),
                        pltpu.VMEM((1, BRANCH), F32)],
        compiler_params=_cparams(("arbitrary", "arbitrary")),
        name="lru_bwd" if rev else "lru_fwd",
    )(u, u, u, cw, cb, wg, ba, bx, lam, h0)


def _heads_l2norm(a):
    outs = []
    for h in range(GDN_HEADS):
        ah = a[:, h * GDN_DH:(h + 1) * GDN_DH]
        outs.append(ah * lax.rsqrt(jnp.sum(ah * ah, axis=-1, keepdims=True) + L2_EPS))
    return jnp.concatenate(outs, axis=1)


def _bdot(a, b):
    return jnp.einsum('ncs,nst->nct', a.astype(BF16), b.astype(BF16), preferred_element_type=F32)


def _gdn_prep_kernel(nt, x_ref, xp_ref, xn_ref, cw_ref, o_ref):
    ti = pl.program_id(0) % nt
    qkv = _dwconv(x_ref[...].astype(F32), _halo(xp_ref, ti == 0), _halo(xn_ref, ti == nt - 1), cw_ref[...])
    qkv = _silu(qkv)
    o_ref[:, :BRANCH] = (_heads_l2norm(qkv[:, :BRANCH]) * (GDN_DH ** -0.5)).astype(BF16)
    o_ref[:, BRANCH:2 * BRANCH] = _heads_l2norm(qkv[:, BRANCH:2 * BRANCH]).astype(BF16)
    o_ref[:, 2 * BRANCH:] = qkv[:, 2 * BRANCH:].astype(BF16)


def _gdn_prep(u, bsz, n, tt, cw):
    nt = n // tt
    r8 = tt // V7X_SUBLANES
    width = 3 * BRANCH
    cb = COL_GDN // width
    last8 = bsz * nt * r8 - 1
    return pl.pallas_call(
        functools.partial(_gdn_prep_kernel, nt),
        grid=(bsz * nt,),
        in_specs=[pl.BlockSpec((tt, width), lambda i: (i, cb)),
                  pl.BlockSpec((V7X_SUBLANES, width), lambda i: (jnp.maximum(i * r8 - 1, 0), cb)),
                  pl.BlockSpec((V7X_SUBLANES, width), lambda i: (jnp.minimum((i + 1) * r8, last8), cb)),
                  pl.BlockSpec((CONV_W, width), lambda i: (0, 0))],
        out_specs=pl.BlockSpec((tt, width), lambda i: (i, 0)),
        out_shape=jax.ShapeDtypeStruct((bsz * n, width), BF16),
        compiler_params=_cparams(("arbitrary",)),
        name="gdn_prep",
    )(u, u, u, cw)


def _gdn_kernel(rev, x_ref, ab_ref, alog_ref, dtb_ref, s0_ref,
                o_ref, sf_ref, kw_s, ku_s, qe_s, gl_s, o_s, s_s):
    i = pl.program_id(1)
    gbt, tt = x_ref.shape[0], x_ref.shape[1]
    rt = gbt * tt
    nc = tt // CHUNK
    nct = gbt * nc
    srows = GDN_HEADS * GDN_DH

    @pl.when(i == 0)
    def _():
        s_s[...] = s0_ref[...].reshape(gbt * srows, GDN_DH)

    x = x_ref[...].reshape(rt, 3 * BRANCH)
    q = x[:, :BRANCH].astype(F32)
    k = x[:, BRANCH:2 * BRANCH].astype(F32)
    v = x[:, 2 * BRANCH:].astype(F32)

    ab = ab_ref[...].reshape(rt, V7X_LANES)
    beta_c = _sigmoid(ab)
    gc = -jnp.exp(alog_ref[...]) * _softplus(ab + dtb_ref[...])
    rowc = lax.broadcasted_iota(jnp.int32, (rt, V7X_LANES), 0) & (CHUNK - 1)
    for s in (1 << b for b in range(LOG2_CHUNK)):
        if rev:
            sh, valid = pltpu.roll(gc, rt - s, 0), rowc < CHUNK - s
        else:
            sh, valid = pltpu.roll(gc, s, 0), rowc >= s
        gc = gc + jnp.where(valid, sh, 0.0)
    gcr_c = gc.reshape(nct, CHUNK, V7X_LANES)
    glr_c = gcr_c[:, 0:1, :] if rev else gcr_c[:, CHUNK - 1:CHUNK, :]
    ekd_c = jnp.exp(glr_c - gcr_c).reshape(rt, V7X_LANES)
    egl_c = jnp.broadcast_to(jnp.exp(glr_c), (nct, V7X_SUBLANES, V7X_LANES)).reshape(nct * V7X_SUBLANES, V7X_LANES)

    boff, aoff = _gdn_gate_cols(rev)

    def spread(a, off):
        return jnp.concatenate(
            [jnp.broadcast_to(a[:, off + h:off + h + 1], (a.shape[0], GDN_DH)) for h in range(GDN_HEADS)], axis=1)

    beta = spread(beta_c, boff)
    gcb = spread(gc, aoff)
    eg = spread(jnp.exp(gc), aoff)
    kd = k * spread(ekd_c, aoff)
    gl_s[...] = spread(egl_c, aoff).reshape(nct, V7X_SUBLANES, BRANCH)
    kb = k * beta
    qd = q * eg
    rhs_v = v * beta
    rhs_k = kb * eg

    ci = lax.broadcasted_iota(jnp.int32, (CHUNK, CHUNK), 0)
    si = lax.broadcasted_iota(jnp.int32, (CHUNK, CHUNK), 1)
    incl = (ci <= si) if rev else (ci >= si)
    strict = (ci < si) if rev else (ci > si)
    eye = (ci == si).astype(F32)
    same_blk = [(ci >> (3 + j)) == (si >> (3 + j)) for j in range(LOG2_CHUNK - 2)]

    nb = GDN_HEADS * nct

    def batched(a, col0=0):
        return jnp.concatenate(
            [a[:, col0 + h * GDN_DH:col0 + (h + 1) * GDN_DH].reshape(nct, CHUNK, GDN_DH) for h in range(GDN_HEADS)],
            axis=0)

    k_b = batched(x, BRANCH)
    q_b = batched(x)
    lhs = jnp.concatenate([batched(kb).astype(BF16), q_b], axis=1)
    kkqk = jnp.einsum('ncd,nsd->ncs', lhs, k_b, preferred_element_type=F32)
    gch = batched(gcb)
    gcol = gch[:, :, :CHUNK]
    grow = jnp.stack([gch[g].T[:CHUNK] for g in range(nb)], axis=0)
    diff = gcol - grow
    decay = jnp.where(incl, jnp.exp(jnp.where(incl, diff, 0.0)), 0.0)
    a_low = jnp.where(strict, kkqk[:, :CHUNK] * decay, 0.0)
    attn = jnp.where(incl, kkqk[:, CHUNK:] * decay, 0.0)
    d_blk = jnp.where(same_blk[0], a_low, 0.0)
    tinv = eye - d_blk
    m = d_blk
    for _ in range(2):
        m = _bdot(m, m)
        tinv = tinv + _bdot(tinv, m)
    for lvl in range(LOG2_CHUNK - 3):
        a_off = jnp.where(jnp.logical_and(same_blk[lvl + 1], jnp.logical_not(same_blk[lvl])), a_low, 0.0)
        tinv = tinv - _bdot(_bdot(tinv, a_off), tinv)
    rhs = jnp.concatenate([batched(rhs_v), batched(rhs_k)], axis=2)
    sol = rhs + _bdot(tinv - eye, rhs)
    kd_b = batched(kd)
    kdt = jnp.stack([kd_b[g].T for g in range(nb)], axis=0)
    m2 = _bdot(jnp.concatenate([kdt, attn], axis=1), sol)
    ku_s[...] = m2[:, :GDN_DH, :GDN_DH].reshape(GDN_HEADS, nct, GDN_DH, GDN_DH)
    kw_s[...] = m2[:, :GDN_DH, GDN_DH:].astype(BF16).reshape(GDN_HEADS, nct, GDN_DH, GDN_DH)
    qe = batched(qd) - m2[:, GDN_DH:, GDN_DH:]
    for h in range(GDN_HEADS):
        ls = slice(h * GDN_DH, (h + 1) * GDN_DH)
        o_s[:, ls] = m2[h * nct:(h + 1) * nct, GDN_DH:, :GDN_DH].reshape(rt, GDN_DH)
        qe_s[:, ls] = qe[h * nct:(h + 1) * nct].reshape(rt, GDN_DH).astype(BF16)

    def chunk_body(j, carry):
        for bb in range(gbt):
            cidx = bb * nc + ((nc - 1 - j) if rev else j)
            r0 = pl.multiple_of(cidx * CHUNK, CHUNK)
            for h in range(GDN_HEADS):
                ls = slice(h * GDN_DH, (h + 1) * GDN_DH)
                ss = slice(bb * srows + h * GDN_DH, bb * srows + (h + 1) * GDN_DH)
                s_h = s_s[ss, :]
                s_b = s_h.astype(BF16)
                o_s[pl.ds(r0, CHUNK), ls] += jnp.dot(qe_s[pl.ds(r0, CHUNK), ls], s_b, preferred_element_type=F32)
                gl = gl_s[cidx][0:1, ls]
                s_s[ss, :] = (s_h * gl - jnp.dot(kw_s[h, cidx], s_b, preferred_element_type=F32)) + ku_s[h, cidx]
        return carry

    lax.fori_loop(0, nc, chunk_body, 0)
    o_ref[...] = o_s[...].astype(BF16).reshape(gbt, tt, BRANCH)
    sf_ref[...] = s_s[...].reshape(gbt, srows, GDN_DH)


def _gdn(xn, ab, rev, bsz, n, tt, alog_c, dtb_c, s0):
    nt = n // tt
    gbt = 2 if bsz % 2 == 0 else 1
    nct = gbt * (tt // CHUNK)
    rt = gbt * tt

    def tix(i):
        return (nt - 1 - i) if rev else i

    full = lambda shape: pl.BlockSpec(shape, lambda b, i: (0,) * len(shape))
    state = pl.BlockSpec((gbt, GDN_HEADS * GDN_DH, GDN_DH), lambda b, i: (b, 0, 0))
    o, sf = pl.pallas_call(
        functools.partial(_gdn_kernel, rev),
        grid=(bsz // gbt, nt),
        in_specs=[pl.BlockSpec((gbt, tt, 3 * BRANCH), lambda b, i: (b, tix(i), 0)),
                  pl.BlockSpec((gbt, tt, V7X_LANES), lambda b, i: (b, tix(i), 0)),
                  full((1, V7X_LANES)), full((1, V7X_LANES)), state],
        out_specs=[pl.BlockSpec((gbt, tt, BRANCH), lambda b, i: (b, tix(i), 0)), state],
        out_shape=[jax.ShapeDtypeStruct((bsz, n, BRANCH), BF16),
                   jax.ShapeDtypeStruct((bsz, GDN_HEADS * GDN_DH, GDN_DH), F32)],
        scratch_shapes=[pltpu.VMEM((GDN_HEADS, nct, GDN_DH, GDN_DH), BF16),
                        pltpu.VMEM((GDN_HEADS, nct, GDN_DH, GDN_DH), F32),
                        pltpu.VMEM((rt, BRANCH), BF16),
                        pltpu.VMEM((nct, V7X_SUBLANES, BRANCH), F32),
                        pltpu.VMEM((rt, BRANCH), F32),
                        pltpu.VMEM((gbt * GDN_HEADS * GDN_DH, GDN_DH), F32)],
        compiler_params=_cparams(("arbitrary", "arbitrary")),
        name="gdn_bwd" if rev else "gdn_fwd",
    )(xn.reshape(bsz, n, 3 * BRANCH), ab.reshape(bsz, n, V7X_LANES), alog_c, dtb_c, s0)
    return o.reshape(bsz * n, BRANCH), sf


def _rope(x, cos, sin_signed):
    lane = lax.broadcasted_iota(jnp.int32, x.shape, 1)
    first = (lane & 31) < 16
    w = x.shape[1]
    rot = jnp.where(first, pltpu.roll(x, w - 16, 1), pltpu.roll(x, 16, 1))
    return x * cos + rot * sin_signed


def _ret_kernel(rev, rope, *refs):
    if rope:
        (q_ref, k_ref, v_ref, cos_ref, sin_ref, dm_ref, qsc_ref, ksc_ref, gch_ref, s0_ref,
         o_ref, sf_ref, s_s) = refs
    else:
        (q_ref, k_ref, v_ref, dm_ref, qsc_ref, ksc_ref, gch_ref, s0_ref, o_ref, sf_ref, s_s) = refs
    i = pl.program_id(1)
    tt = q_ref.shape[0]
    nc = tt // RET_CHUNK

    @pl.when(i == 0)
    def _():
        s_s[...] = s0_ref[...]

    q = q_ref[...].astype(F32)
    k = k_ref[...].astype(F32)
    if rope:
        q = _rope(q, cos_ref[...], sin_ref[...])
        k = _rope(k, cos_ref[...], sin_ref[...])
    k = k * (RET_QK ** -0.5)
    q3 = q.reshape(nc, RET_CHUNK, RET_HEADS * RET_QK)
    k3 = k.reshape(nc, RET_CHUNK, RET_HEADS * RET_QK)
    qd3 = q3 * qsc_ref[...]
    kd3 = k3 * ksc_ref[...]
    v = v_ref[...]
    lane = lax.broadcasted_iota(jnp.int32, (nc, RET_CHUNK, V7X_LANES), 2)
    lo_half = lane < RET_QK

    def pair_lanes(a3, h, keep_other=False):
        pair = a3[:, :, (h // 2) * V7X_LANES:(h // 2 + 1) * V7X_LANES]
        if keep_other:
            return pair
        sel = lo_half if h % 2 == 0 else jnp.logical_not(lo_half)
        return jnp.where(sel, pair, 0.0)

    def batched(fn):
        return jnp.concatenate([fn(h) for h in range(RET_HEADS)], axis=0)

    nb = RET_HEADS * nc
    q_b = batched(lambda h: pair_lanes(q3, h)).astype(BF16)
    k_b = batched(lambda h: pair_lanes(k3, h, keep_other=True)).astype(BF16)
    v_b = batched(lambda h: v[:, h * RET_V:(h + 1) * RET_V].reshape(nc, RET_CHUNK, RET_V))
    dm_b = batched(lambda h: jnp.broadcast_to(dm_ref[h], (nc, RET_CHUNK, RET_CHUNK)))
    scores = jnp.einsum('ncd,nsd->ncs', q_b, k_b, preferred_element_type=F32) * dm_b
    o_intra = _bdot(scores, v_b)
    kd_b = batched(lambda h: pair_lanes(kd3, h))
    kdt = jnp.stack([kd_b[g].T for g in range(nb)], axis=0)
    kv = _bdot(kdt, v_b)
    qd_b = batched(lambda h: pair_lanes(qd3, h)).astype(BF16)

    order = range(nc - 1, -1, -1) if rev else range(nc)
    s_in = [None] * nb
    for h in range(RET_HEADS):
        gch = gch_ref[0:1, h * RET_V:(h + 1) * RET_V]
        s_h = s_s[h * V7X_LANES:(h + 1) * V7X_LANES, :]
        for n in order:
            s_in[h * nc + n] = s_h
            s_h = s_h * gch + kv[h * nc + n]
        s_s[h * V7X_LANES:(h + 1) * V7X_LANES, :] = s_h
    o = o_intra + _bdot(qd_b, jnp.stack(s_in, axis=0))
    for h in range(RET_HEADS):
        o_ref[:, h * RET_V:(h + 1) * RET_V] = o[h * nc:(h + 1) * nc].reshape(tt, RET_V).astype(BF16)
    sf_ref[...] = s_s[...]


def _ret(u, rev, bsz, n, tt, tabs, rope_tabs, s0):
    nt = n // tt
    dm, qsc, ksc, gch = tabs

    def tix(i):
        return (nt - 1 - i) if rev else i

    qk_w = RET_HEADS * RET_QK
    full = lambda shape: pl.BlockSpec(shape, lambda b, i: (0,) * len(shape))
    state = pl.BlockSpec((None, RET_HEADS * V7X_LANES, RET_V), lambda b, i: (b, 0, 0))
    in_specs = [pl.BlockSpec((tt, qk_w), lambda b, i: (b * nt + tix(i), COL_RETQ // qk_w)),
                pl.BlockSpec((tt, qk_w), lambda b, i: (b * nt + tix(i), COL_RETK // qk_w)),
                pl.BlockSpec((tt, BRANCH), lambda b, i: (b * nt + tix(i), COL_RETV // BRANCH))]
    args = [u, u, u]
    if rope_tabs is not None:
        in_specs += [pl.BlockSpec((tt, qk_w), lambda b, i: (tix(i), 0))] * 2
        args += list(rope_tabs)
    in_specs += [full(dm.shape), full(qsc.shape), full(ksc.shape), full(gch.shape), state]
    args += [dm, qsc, ksc, gch, s0]
    return pl.pallas_call(
        functools.partial(_ret_kernel, rev, rope_tabs is not None),
        grid=(bsz, nt),
        in_specs=in_specs,
        out_specs=[pl.BlockSpec((tt, BRANCH), lambda b, i: (b * nt + tix(i), 0)), state],
        out_shape=[jax.ShapeDtypeStruct((bsz * n, BRANCH), BF16),
                   jax.ShapeDtypeStruct((bsz, RET_HEADS * V7X_LANES, RET_V), F32)],
        scratch_shapes=[pltpu.VMEM((RET_HEADS * V7X_LANES, RET_V), F32)],
        compiler_params=_cparams(("arbitrary", "arbitrary")),
        name="ret_bwd" if rev else "ret_fwd",
    )(*args)


def _head_rms(a):
    outs = []
    for h in range(BRANCH // V7X_LANES):
        ah = a[:, h * V7X_LANES:(h + 1) * V7X_LANES]
        outs.append(ah * lax.rsqrt(jnp.mean(ah * ah, axis=-1, keepdims=True) + NORM_EPS))
    return jnp.concatenate(outs, axis=1)


def _merge_kernel(final, ona_ref, lf_ref, lb_ref, gf_ref, gb_ref, rf_ref, rb_ref, z_ref, mg_ref, x_ref,
                  gate_ref, gng_ref, wb_ref, wo_ref, fg_ref, o_ref):
    sz = _silu(z_ref[...].astype(F32))
    ya = ona_ref[...].astype(F32) * sz[:, :BRANCH]
    yb = (lf_ref[...].astype(F32) + lb_ref[...].astype(F32)) * sz[:, BRANCH:2 * BRANCH]
    oc = gf_ref[...].astype(F32) + gb_ref[...].astype(F32)
    yc = _head_rms(oc) * gng_ref[...] * sz[:, 2 * BRANCH:3 * BRANCH]
    od = rf_ref[...].astype(F32) + rb_ref[...].astype(F32)
    yd = _head_rms(od) * sz[:, 3 * BRANCH:]
    acc = None
    for idx, y in enumerate((ya, yb, yc, yd)):
        proj = jnp.dot(y.astype(BF16), wb_ref[idx], preferred_element_type=F32)
        gate = _sigmoid(mg_ref[:, idx * D_MODEL:(idx + 1) * D_MODEL].astype(F32))
        acc = gate * proj if acc is None else acc + gate * proj
    out = jnp.dot(acc.astype(BF16), wo_ref[...], preferred_element_type=F32)
    xn = x_ref[...] + gate_ref[...] * out
    if final:
        ms = jnp.mean(xn * xn, axis=-1, keepdims=True)
        xn = xn * lax.rsqrt(ms + NORM_EPS) * fg_ref[...]
    o_ref[...] = xn


def _merge(final, o_na, lru, gdn, ret, u, x2, gate, gng, wb, wo, fg, tm, tiles_per_mod):
    t = x2.shape[0]
    br = pl.BlockSpec((tm, BRANCH), lambda i: (i, 0))
    full = lambda shape: pl.BlockSpec(shape, lambda i: (0,) * len(shape))
    return pl.pallas_call(
        functools.partial(_merge_kernel, final),
        grid=(t // tm,),
        in_specs=[br] * 7 + [
            pl.BlockSpec((tm, 4 * BRANCH), lambda i: (i, COL_Z // (4 * BRANCH))),
            pl.BlockSpec((tm, N_BRANCH * D_MODEL), lambda i: (i, 0)),
            pl.BlockSpec((tm, D_MODEL), lambda i: (i, 0)),
            pl.BlockSpec((None, 1, D_MODEL), lambda i: (i // tiles_per_mod, 0, 0)),
            full((1, BRANCH)), full((N_BRANCH, BRANCH, D_MODEL)), full((D_MODEL, D_MODEL)),
            full((1, D_MODEL))],
        out_specs=pl.BlockSpec((tm, D_MODEL), lambda i: (i, 0)),
        out_shape=jax.ShapeDtypeStruct((t, D_MODEL), F32),
        compiler_params=_cparams(("arbitrary",)),
        name="merge",
    )(o_na, lru[0], lru[1], gdn[0], gdn[1], ret[0], ret[1], u, u, x2, gate, gng, wb, wo, fg)


def _permute_w_in(w):
    def cols(name):
        o, n = _SRC[name]
        return w[:, o:o + n]
    w_p = jnp.concatenate([cols('merge'), cols('na_z'), cols('lru_z'), cols('gdn_z'), cols('ret_z'),
                           cols('gdn_qkv'), cols('na_q'), cols('na_k'), cols('na_v'), cols('lru_x'),
                           cols('ret_q'), cols('ret_k'), cols('ret_v')], axis=1)
    w_ab = jnp.pad(cols('gdn_ab'), ((0, 0), (0, V7X_LANES - 4 * GDN_HEADS)))
    return w_p.astype(BF16), w_ab.astype(BF16)


def _na_bias_tables(rpb):
    col = np.arange(GRID_W)
    col_start = np.clip(col - NA_WIN_COLS // 2, 0, GRID_W - NA_WIN_COLS)
    in_win = (col[None, :] >= col_start[:, None]) & (col[None, :] < col_start[:, None] + NA_WIN_COLS)
    dc = np.clip(col[None, :] - col[:, None] + NA_WIN_COLS - 1, 0, 2 * NA_WIN_COLS - 2)
    onehot = (np.arange(2 * NA_WIN_COLS - 1)[:, None, None] == dc[None]).astype(np.float32)
    t = jnp.einsum('hdj,jqk->hdqk', rpb.astype(F32), onehot, precision=lax.Precision.HIGHEST)
    t = jnp.where(in_win[None, None], t, NEG_INF)
    tv = jnp.stack([t[:, ds:ds + NA_WIN_ROWS] for ds in range(NA_WIN_ROWS)], axis=0)
    return jnp.transpose(tv, (0, 1, 3, 2, 4)).reshape(NA_WIN_ROWS, NA_HEADS, GRID_W, NA_WIN_ROWS * GRID_W)


def _lru_gate_weights(wa, wx):
    def chunk(w, c):
        z = jnp.zeros((LRU_BLOCK, LRU_BLOCK), w.dtype)
        return jnp.concatenate([jnp.concatenate([w[2 * c], z], axis=1),
                                jnp.concatenate([z, w[2 * c + 1]], axis=1)], axis=0)
    return jnp.stack([jnp.concatenate([chunk(wa, c), chunk(wx, c)], axis=1)
                      for c in range(LRU_BLOCKS // 2)], axis=0).astype(BF16)


def _rope_tables(n):
    t = np.arange(n)
    row = (t // GRID_W).astype(np.float32)
    col = (t % GRID_W).astype(np.float32)
    quarter = RET_QK // 4
    inv = jnp.asarray(ROPE_BASE, F32) ** (-jnp.arange(quarter, dtype=F32) / quarter)
    ang_r = jnp.asarray(row)[:, None] * inv[None]
    ang_c = jnp.asarray(col)[:, None] * inv[None]
    ang = jnp.concatenate([ang_r, ang_r, ang_c, ang_c], axis=-1)
    cos, sin = jnp.cos(ang), jnp.sin(ang)
    sign = np.where((np.arange(RET_QK) % 32) < 16, -1.0, 1.0).astype(np.float32)
    return jnp.tile(cos, (1, RET_HEADS)), jnp.tile(sin * sign[None], (1, RET_HEADS))


def _ret_tables(rev):
    log_gamma = jnp.log1p(-jnp.exp2(-(5.0 + jnp.arange(RET_HEADS, dtype=F32))))
    pos = jnp.arange(RET_CHUNK, dtype=F32)
    if rev:
        pos = pos[::-1]
    rel = pos[:, None] - pos[None, :]
    dm = jnp.where(rel >= 0, jnp.exp(jnp.maximum(rel, 0.0)[None] * log_gamma[:, None, None]), 0.0)
    ksc = jnp.exp((RET_CHUNK - 1 - pos)[None] * log_gamma[:, None])
    qsc = jnp.exp((pos + 1.0)[None] * log_gamma[:, None])
    widen = lambda a: jnp.repeat(a.T, RET_QK, axis=1)
    gch = jnp.broadcast_to(jnp.repeat(jnp.exp(RET_CHUNK * log_gamma), RET_V)[None],
                           (V7X_SUBLANES, RET_HEADS * RET_V))
    return dm, widen(qsc), widen(ksc), gch


def _gdn_gate_cols(rev):
    return (GDN_HEADS if rev else 0), (3 * GDN_HEADS if rev else 2 * GDN_HEADS)


def _gdn_gate_row(vals, rev):
    _, aoff = _gdn_gate_cols(rev)
    return jnp.zeros((1, V7X_LANES), F32).at[0, aoff:aoff + GDN_HEADS].set(vals.astype(F32))


def _layer(x2, c2, bsz, n, n_ctx, mod, p, consts, with_ctx_out, final):
    shift_x = mod[:bsz, None, :D_MODEL]
    scale_x = mod[:bsz, None, D_MODEL:2 * D_MODEL]
    gate_x = mod[:bsz, None, 2 * D_MODEL:]
    shift_c = mod[bsz:bsz + 1, None, :D_MODEL]
    scale_c = mod[bsz:bsz + 1, None, D_MODEL:2 * D_MODEL]
    gate_c = mod[bsz:bsz + 1, None, 2 * D_MODEL:]

    tm = 1024 if n % 1024 == 0 else 512
    tmc = min(1024, bsz * n_ctx)
    tt = 512
    tt_ret = 1024 if n % 1024 == 0 else 512
    big = 1 << 30
    u, ab = _inproj(x2, p['norm_g'], shift_x, scale_x, p['w_p'], p['w_ab'], tm, n // tm)
    uc, abc = _inproj(c2, p['norm_g'], shift_c, scale_c, p['w_p'], p['w_ab'], tmc, big)

    o_na = _na_latent(u, uc, p['na_grp'], bsz, n, n_ctx)
    gdn_x = _gdn_prep(u, bsz, n, tt, p['gdn_cw'])
    gdn_xc = _gdn_prep(uc, bsz, n_ctx, n_ctx, p['gdn_cw'])

    lru_o, lru_oc = [], []
    gdn_o, gdn_oc = [], []
    ret_o, ret_oc = [], []
    for d, rev in enumerate((False, True)):
        h0 = jnp.zeros((bsz, 1, BRANCH), F32)
        lru_args = (p['lru_cw'], p['lru_cb'], p['lru_wg'][d], p['lru_ba'][d], p['lru_bx'][d], p['lru_lam'][d])
        oc_, hc = _lru(uc, rev, bsz, n_ctx, n_ctx, *lru_args, h0)
        ol_, _ = _lru(u, rev, bsz, n, tt_ret, *lru_args, hc)
        lru_o.append(ol_)
        lru_oc.append(oc_)

        s0 = jnp.zeros((bsz, GDN_HEADS * GDN_DH, GDN_DH), F32)
        gdn_args = (p['gdn_alog'][d], p['gdn_dtb'][d])
        oc_, sc = _gdn(gdn_xc, abc, rev, bsz, n_ctx, n_ctx, *gdn_args, s0)
        ol_, _ = _gdn(gdn_x, ab, rev, bsz, n, tt, *gdn_args, sc)
        gdn_o.append(ol_)
        gdn_oc.append(oc_)

        r0 = jnp.zeros((bsz, RET_HEADS * V7X_LANES, RET_V), F32)
        oc_, rc = _ret(uc, rev, bsz, n_ctx, n_ctx, consts['ret_tabs'][d], None, r0)
        ol_, _ = _ret(u, rev, bsz, n, tt_ret, consts['ret_tabs'][d], consts['rope'], rc)
        ret_o.append(ol_)
        ret_oc.append(oc_)

    merge_w = (p['gdn_ng'], p['w_branch'], p['w_out'], consts['final_g'])
    x_new = _merge(final, o_na, lru_o, gdn_o, ret_o, u, x2, gate_x, *merge_w, 512, n // 512)
    c_new = None
    if with_ctx_out:
        o_nac = _na_ctx(uc, bsz, n_ctx)
        c_new = _merge(False, o_nac, lru_oc, gdn_oc, ret_oc, uc, c2, gate_c, *merge_w, n_ctx, big)
    return x_new, c_new


def kernel(x, c, ctx, c_ctx, norm_g, w_mod, b_mod, w_in, na_rpb, lru_conv_w, lru_conv_b, lru_wa, lru_ba,
           lru_wx, lru_bx, lru_lam, gdn_conv_w, gdn_a_log, gdn_dt_bias, gdn_norm_g, w_branch, w_out,
           final_norm_g):
    bsz, n, d = x.shape
    n_ctx = ctx.shape[1]
    depth = w_in.shape[0]
    assert d == D_MODEL and w_in.shape[2] == D_IN
    assert n % 512 == 0 and n_ctx % max(CHUNK, RET_CHUNK) == 0 and (bsz * n_ctx) % 8 == 0

    rows = -(-(bsz + 1) // V7X_SUBLANES) * V7X_SUBLANES
    cc = jnp.zeros((rows, D_MODEL), F32).at[:bsz].set(c).at[bsz].set(c_ctx)
    mods = _modulation(cc, w_mod, b_mod)

    consts = {
        'rope': _rope_tables(n),
        'ret_tabs': (_ret_tables(False), _ret_tables(True)),
        'final_g': final_norm_g.reshape(1, D_MODEL),
    }

    x2 = x.reshape(bsz * n, D_MODEL)
    c2 = ctx.reshape(bsz * n_ctx, D_MODEL)
    for layer in range(depth):
        w_p, w_ab = _permute_w_in(w_in[layer])
        p = {
            'na_grp': _na_bias_tables(na_rpb[layer]),
            'norm_g': norm_g[layer].reshape(1, D_MODEL),
            'w_p': w_p, 'w_ab': w_ab,
            'lru_cw': lru_conv_w[layer], 'lru_cb': lru_conv_b[layer].reshape(1, BRANCH),
            'lru_wg': [_lru_gate_weights(lru_wa[layer, dd], lru_wx[layer, dd]) for dd in range(2)],
            'lru_ba': [lru_ba[layer, dd].reshape(1, BRANCH) for dd in range(2)],
            'lru_bx': [lru_bx[layer, dd].reshape(1, BRANCH) for dd in range(2)],
            'lru_lam': [lru_lam[layer, dd].reshape(1, BRANCH) for dd in range(2)],
            'gdn_cw': gdn_conv_w[layer],
            'gdn_alog': [_gdn_gate_row(gdn_a_log[layer, dd], dd == 1) for dd in range(2)],
            'gdn_dtb': [_gdn_gate_row(gdn_dt_bias[layer, dd], dd == 1) for dd in range(2)],
            'gdn_ng': jnp.tile(gdn_norm_g[layer].astype(F32), GDN_HEADS)[None],
            'w_branch': w_branch[layer].astype(BF16),
            'w_out': w_out[layer].astype(BF16),
        }
        last = layer == depth - 1
        x2, c2 = _layer(x2, c2, bsz, n, n_ctx, mods[layer], p, consts, not last, last)
    return x2.reshape(bsz, n, D_MODEL)
```

```python
import functools
import math

import numpy as np
import jax
import jax.numpy as jnp
from jax import lax
from jax.experimental import pallas as pl
from jax.experimental.pallas import tpu as pltpu

F32 = jnp.float32
BF16 = jnp.bfloat16

D_MODEL = 1024
GRID_W = 64
BRANCH = 512
N_BRANCH = 4
NA_HEADS = 8
NA_DH = 64
NA_WIN_ROWS = 8
NA_WIN_COLS = 16
LRU_BLOCKS = 8
LRU_BLOCK = 64
LRU_C = 8.0
CONV_W = 4
GDN_HEADS = 4
GDN_DH = 128
RET_HEADS = 4
RET_QK = 64
RET_V = 128
CHUNK = 64
LOG2_CHUNK = 6
assert 1 << LOG2_CHUNK == CHUNK
RET_CHUNK = 128
ROPE_BASE = 10000.0
NORM_EPS = 1e-6
L2_EPS = 1e-6
NEG_INF = -1e30

V7X_LANES = 128
V7X_SUBLANES = 8
VMEM_LIMIT = 56 * 1024 * 1024

COL_MERGE = 0
COL_Z = 4096
COL_GDN = 6144
COL_NAQ = 7680
COL_NAK = 8192
COL_NAV = 8704
COL_LRU = 9216
COL_RETQ = 9728
COL_RETK = 9984
COL_RETV = 10240
U_COLS = 10752
IN_TN = 3584

_SRC = {}
_off = 0
for _name, _w in (('na_q', 512), ('na_k', 512), ('na_v', 512), ('na_z', 512), ('lru_x', 512),
                  ('lru_z', 512), ('gdn_qkv', 1536), ('gdn_ab', 16), ('gdn_z', 512), ('ret_q', 256),
                  ('ret_k', 256), ('ret_v', 512), ('ret_z', 512), ('merge', 4096)):
    _SRC[_name] = (_off, _w)
    _off += _w
D_IN = _off


def _cparams(sem):
    return pltpu.CompilerParams(dimension_semantics=sem, vmem_limit_bytes=VMEM_LIMIT)


def _softplus(x):
    return jnp.maximum(x, 0.0) + jnp.log1p(jnp.exp(-jnp.abs(x)))


NEG_LOG2_E = -1.4426950408889634


def _sigmoid(x):
    return 1.0 / (1.0 + jnp.exp2(x * NEG_LOG2_E))


def _silu(x):
    return x * _sigmoid(x)


def _mod_kernel(c_ref, w_ref, b_ref, o_ref):
    c = c_ref[...]
    a = _silu(c).astype(BF16)
    o_ref[...] = jnp.dot(a, w_ref[...].astype(BF16), preferred_element_type=F32) + b_ref[...]


def _modulation(cc, w_mod, b_mod):
    depth = w_mod.shape[0]
    rows = cc.shape[0]
    tn = 512
    return pl.pallas_call(
        _mod_kernel,
        grid=(depth, 3 * D_MODEL // tn),
        in_specs=[pl.BlockSpec((rows, D_MODEL), lambda l, j: (0, 0)),
                  pl.BlockSpec((None, D_MODEL, tn), lambda l, j: (l, 0, j)),
                  pl.BlockSpec((None, 1, tn), lambda l, j: (l, 0, j))],
        out_specs=pl.BlockSpec((None, rows, tn), lambda l, j: (l, 0, j)),
        out_shape=jax.ShapeDtypeStruct((depth, rows, 3 * D_MODEL), F32),
        compiler_params=_cparams(("arbitrary", "arbitrary")),
        name="modulation",
    )(cc, w_mod, b_mod.reshape(depth, 1, 3 * D_MODEL))


def _inproj_kernel(x_ref, g_ref, sh_ref, sc_ref, w_ref, wab_ref, u_ref, ab_ref, xn_ref):
    @pl.when(pl.program_id(1) == 0)
    def _():
        x = x_ref[...]
        ms = jnp.mean(x * x, axis=-1, keepdims=True)
        xn = x * lax.rsqrt(ms + NORM_EPS) * g_ref[...]
        xn = xn * (1.0 + sc_ref[...]) + sh_ref[...]
        xb = xn.astype(BF16)
        xn_ref[...] = xb
        ab_ref[...] = jnp.dot(xb, wab_ref[...], preferred_element_type=F32)

    u_ref[...] = jnp.dot(xn_ref[...], w_ref[...], preferred_element_type=F32).astype(BF16)


def _inproj(x2, g, shift, scale, w_p, w_ab, tm, tiles_per_mod):
    t = x2.shape[0]
    return pl.pallas_call(
        _inproj_kernel,
        grid=(t // tm, U_COLS // IN_TN),
        in_specs=[pl.BlockSpec((tm, D_MODEL), lambda i, j: (i, 0)),
                  pl.BlockSpec((1, D_MODEL), lambda i, j: (0, 0)),
                  pl.BlockSpec((None, 1, D_MODEL), lambda i, j: (i // tiles_per_mod, 0, 0)),
                  pl.BlockSpec((None, 1, D_MODEL), lambda i, j: (i // tiles_per_mod, 0, 0)),
                  pl.BlockSpec((D_MODEL, IN_TN), lambda i, j: (0, j)),
                  pl.BlockSpec((D_MODEL, V7X_LANES), lambda i, j: (0, 0))],
        out_specs=[pl.BlockSpec((tm, IN_TN), lambda i, j: (i, j)),
                   pl.BlockSpec((tm, V7X_LANES), lambda i, j: (i, 0))],
        out_shape=[jax.ShapeDtypeStruct((t, U_COLS), BF16),
                   jax.ShapeDtypeStruct((t, V7X_LANES), F32)],
        scratch_shapes=[pltpu.VMEM((tm, D_MODEL), BF16)],
        compiler_params=_cparams(("arbitrary", "arbitrary")),
        name="inproj",
    )(x2, g, shift, scale, w_p, w_ab)


NA_GROUP_ROWS = 4
NA_GROUP_KEYS = (2 * (NA_GROUP_ROWS - 1) + NA_WIN_ROWS) * GRID_W


def _na_softmax_pv(qm, kw, vw, kc, vc, bias):
    s_lat = lax.dot_general(qm, kw, (((1,), (1,)), ((), ())), preferred_element_type=F32) + bias
    s_ctx = lax.dot_general(qm, kc, (((1,), (1,)), ((), ())), preferred_element_type=F32)
    m = jnp.maximum(jnp.max(s_lat, axis=-1, keepdims=True), jnp.max(s_ctx, axis=-1, keepdims=True))
    e_lat = jnp.exp(s_lat - m)
    e_ctx = jnp.exp(s_ctx - m)
    den = jnp.sum(e_lat, axis=-1, keepdims=True) + jnp.sum(e_ctx, axis=-1, keepdims=True)
    o = jnp.dot(e_lat.astype(BF16), vw, preferred_element_type=F32)
    o = o + jnp.dot(e_ctx.astype(BF16), vc, preferred_element_type=F32)
    return o / den


def _na_row_geometry(case, par, m, rows):
    if case == 1:
        return 2 * m, NA_WIN_ROWS - 1 - NA_WIN_ROWS // 2
    first_row, win_start = (0, 0) if case == 0 else (rows - 8, rows - 14)
    r = first_row + 2 * m + par
    row_start = min(max(r - NA_WIN_ROWS // 2, 0), rows - NA_WIN_ROWS)
    return row_start - win_start, row_start - r + NA_WIN_ROWS - 1


def _na_build_bias(case, rows, t_ref, gb_s):
    def head_body(h, carry):
        for par in range(2):
            for m in range(NA_GROUP_ROWS):
                off, dstart = _na_row_geometry(case, par, m, rows)
                lo = off * GRID_W
                hi = NA_GROUP_KEYS - lo - NA_WIN_ROWS * GRID_W
                assert lo >= 0 and hi >= 0
                parts = [t_ref[dstart, h]]
                if lo:
                    parts.insert(0, jnp.full((GRID_W, lo), NEG_INF, F32))
                if hi:
                    parts.append(jnp.full((GRID_W, hi), NEG_INF, F32))
                gb_s[par, h, m * GRID_W:(m + 1) * GRID_W, :] = jnp.concatenate(parts, axis=1)
        return carry

    lax.fori_loop(0, NA_HEADS, head_body, 0)


def _na_kernel(nblk, rows, q_ref, k_ref, v_ref, kc_ref, vc_ref, t_ref, o_ref, gb_ref):
    i = pl.program_id(1)

    @pl.when(i == 0)
    def _():
        _na_build_bias(0, rows, t_ref, gb_ref)

    @pl.when(i == 1)
    def _():
        _na_build_bias(1, rows, t_ref, gb_ref)

    @pl.when(i == nblk - 1)
    def _():
        _na_build_bias(2, rows, t_ref, gb_ref)

    gq = NA_GROUP_ROWS * GRID_W
    lane = lax.broadcasted_iota(jnp.int32, (gq, V7X_LANES), 1)
    lo_half = lane < NA_DH
    for par in range(2):
        koff = jnp.where(i == 0, 0, jnp.where(i == nblk - 1, 2 * GRID_W, par * GRID_W))
        koff = pl.multiple_of(koff, GRID_W)
        for p in range(NA_HEADS // 2):
            ls = slice(p * V7X_LANES, (p + 1) * V7X_LANES)
            kc = kc_ref[:, ls]
            vc = vc_ref[:, ls]
            q_g = jnp.concatenate(
                [q_ref[(2 * m + par) * GRID_W:(2 * m + par + 1) * GRID_W, ls] for m in range(NA_GROUP_ROWS)], axis=0)
            q_g = (q_g.astype(F32) * (NA_DH ** -0.5)).astype(BF16)
            kw = k_ref[pl.ds(koff, NA_GROUP_KEYS), ls]
            vw = v_ref[pl.ds(koff, NA_GROUP_KEYS), ls]
            qm2 = jnp.concatenate([jnp.where(lo_half, q_g, jnp.zeros_like(q_g)),
                                   jnp.where(lo_half, jnp.zeros_like(q_g), q_g)], axis=0)
            b2 = jnp.concatenate([gb_ref[par, 2 * p], gb_ref[par, 2 * p + 1]], axis=0)
            o2 = _na_softmax_pv(qm2, kw, vw, kc, vc, b2)
            o_pair = jnp.where(lo_half, o2[:gq], o2[gq:]).astype(BF16)
            for m in range(NA_GROUP_ROWS):
                o_ref[(2 * m + par) * GRID_W:(2 * m + par + 1) * GRID_W, ls] = o_pair[m * GRID_W:(m + 1) * GRID_W]


def _na_latent(u, uc, tvar, bsz, n, n_ctx):
    rows = n // GRID_W
    assert rows >= 16 and rows % 8 == 0
    nblk = rows // 8
    tq = 8 * GRID_W
    win = 16 * GRID_W
    u3 = u.reshape(bsz, n, U_COLS)

    def kv_spec(col):
        return pl.BlockSpec((None, pl.Element(win), pl.Element(BRANCH)),
                            lambda b, i: (b, jnp.clip(8 * i - 4, 0, rows - 16) * GRID_W, col))

    return pl.pallas_call(
        functools.partial(_na_kernel, nblk, rows),
        grid=(bsz, nblk),
        in_specs=[pl.BlockSpec((tq, BRANCH), lambda b, i: (b * nblk + i, COL_NAQ // BRANCH)),
                  kv_spec(COL_NAK), kv_spec(COL_NAV),
                  pl.BlockSpec((n_ctx, BRANCH), lambda b, i: (b, COL_NAK // BRANCH)),
                  pl.BlockSpec((n_ctx, BRANCH), lambda b, i: (b, COL_NAV // BRANCH)),
                  pl.BlockSpec(tvar.shape, lambda b, i: (0, 0, 0, 0), pipeline_mode=pl.Buffered(1))],
        out_specs=pl.BlockSpec((tq, BRANCH), lambda b, i: (b * nblk + i, 0)),
        out_shape=jax.ShapeDtypeStruct((bsz * n, BRANCH), BF16),
        scratch_shapes=[pltpu.VMEM((2, NA_HEADS, NA_GROUP_ROWS * GRID_W, NA_GROUP_KEYS), F32)],
        compiler_params=_cparams(("arbitrary", "arbitrary")),
        name="na_latent",
    )(u, u3, u3, uc, uc, tvar)


def _na_ctx_kernel(q_ref, k_ref, v_ref, o_ref):
    n_ctx = q_ref.shape[0]
    lane = lax.broadcasted_iota(jnp.int32, (n_ctx, V7X_LANES), 1)
    lo_half = lane < NA_DH
    for p in range(NA_HEADS // 2):
        ls = slice(p * V7X_LANES, (p + 1) * V7X_LANES)
        q_pair = (q_ref[:, ls].astype(F32) * (NA_DH ** -0.5)).astype(BF16)
        k = k_ref[:, ls]
        v = v_ref[:, ls]
        outs = []
        for e in range(2):
            sel = lo_half if e == 0 else jnp.logical_not(lo_half)
            qm = jnp.where(sel, q_pair, jnp.zeros_like(q_pair))
            s = lax.dot_general(qm, k, (((1,), (1,)), ((), ())), preferred_element_type=F32)
            m = jnp.max(s, axis=-1, keepdims=True)
            ex = jnp.exp(s - m)
            den = jnp.sum(ex, axis=-1, keepdims=True)
            outs.append(jnp.dot(ex.astype(BF16), v, preferred_element_type=F32) / den)
        o_ref[:, ls] = jnp.where(lo_half, outs[0], outs[1]).astype(BF16)


def _na_ctx(uc, bsz, n_ctx):
    def spec(col):
        return pl.BlockSpec((n_ctx, BRANCH), lambda b: (b, col // BRANCH))
    return pl.pallas_call(
        _na_ctx_kernel,
        grid=(bsz,),
        in_specs=[spec(COL_NAQ), spec(COL_NAK), spec(COL_NAV)],
        out_specs=pl.BlockSpec((n_ctx, BRANCH), lambda b: (b, 0)),
        out_shape=jax.ShapeDtypeStruct((bsz * n_ctx, BRANCH), BF16),
        compiler_params=_cparams(("arbitrary",)),
        name="na_ctx",
    )(uc, uc, uc)


def _dwconv(x, prev8, next8, w):
    tt = x.shape[0]
    w0, w1, w2, w3 = (w[j:j + 1] for j in range(CONV_W))

    def taps(a):
        n = a.shape[0]
        return (w0 * pltpu.roll(a, 2, 0) + w1 * pltpu.roll(a, 1, 0) + w2 * a
                + w3 * pltpu.roll(a, n - 1, 0))

    y = taps(x)
    head = taps(jnp.concatenate([prev8, x[0:16]], axis=0))[8:16]
    tail = taps(jnp.concatenate([x[tt - 16:tt], next8], axis=0))[8:16]
    return jnp.concatenate([head, y[8:tt - 8], tail], axis=0)


def _tile_index(rev):
    i = pl.program_id(1)
    nt = pl.num_programs(1)
    ti = (nt - 1 - i) if rev else i
    return i, ti, nt


def _scan_specs(tt, nt, bsz, rev, col, width):
    r8 = tt // V7X_SUBLANES
    cb = col // width
    last8 = bsz * nt * r8 - 1

    def tix(i):
        return (nt - 1 - i) if rev else i

    main = pl.BlockSpec((tt, width), lambda b, i: (b * nt + tix(i), cb))
    prev = pl.BlockSpec((V7X_SUBLANES, width),
                        lambda b, i: (jnp.maximum((b * nt + tix(i)) * r8 - 1, 0), cb))
    nxt = pl.BlockSpec((V7X_SUBLANES, width),
                       lambda b, i: (jnp.minimum((b * nt + tix(i) + 1) * r8, last8), cb))
    return main, prev, nxt


def _halo(ref, is_edge):
    h = ref[...].astype(F32)
    return jnp.where(is_edge, jnp.zeros_like(h), h)


def _lru_kernel(rev, x_ref, xp_ref, xn_ref, cw_ref, cb_ref, wg_ref, ba_ref, bx_ref, lam_ref, h0_ref,
                o_ref, hf_ref, a_s, b_s, h_s):
    i, ti, nt = _tile_index(rev)
    tt = x_ref.shape[0]
    ng = tt // V7X_SUBLANES

    @pl.when(i == 0)
    def _():
        h_s[...] = h0_ref[...]

    xs = _dwconv(x_ref[...].astype(F32), _halo(xp_ref, ti == 0), _halo(xn_ref, ti == nt - 1),
                 cw_ref[...]) + cb_ref[...]
    xb = xs.astype(BF16)
    pa, px = [], []
    for c in range(BRANCH // V7X_LANES):
        g = jnp.dot(xb[:, c * V7X_LANES:(c + 1) * V7X_LANES], wg_ref[c], preferred_element_type=F32)
        pa.append(g[:, :V7X_LANES])
        px.append(g[:, V7X_LANES:])
    r_gate = _sigmoid(jnp.concatenate(pa, axis=1) + ba_ref[...])
    i_gate = _sigmoid(jnp.concatenate(px, axis=1) + bx_ref[...])
    z = LRU_C * r_gate * _softplus(-lam_ref[...])
    a = jnp.exp(-z)
    th = jnp.tanh(z)
    b = jnp.sqrt(2.0 * th / (1.0 + th)) * (i_gate * xs)

    row = lax.broadcasted_iota(jnp.int32, (tt, BRANCH), 0) & (V7X_SUBLANES - 1)
    for s in (1, 2, 4):
        if rev:
            a_sh, b_sh, valid = pltpu.roll(a, tt - s, 0), pltpu.roll(b, tt - s, 0), row < V7X_SUBLANES - s
        else:
            a_sh, b_sh, valid = pltpu.roll(a, s, 0), pltpu.roll(b, s, 0), row >= s
        b = jnp.where(valid, b + a * b_sh, b)
        a = jnp.where(valid, a * a_sh, a)
    a_s[...] = a
    b_s[...] = b

    def body(j, h):
        jj = (ng - 1 - j) if rev else j
        r0 = pl.multiple_of(jj * V7X_SUBLANES, V7X_SUBLANES)
        hb = a_s[pl.ds(r0, V7X_SUBLANES), :] * h + b_s[pl.ds(r0, V7X_SUBLANES), :]
        b_s[pl.ds(r0, V7X_SUBLANES), :] = hb
        return hb[0:1] if rev else hb[V7X_SUBLANES - 1:V7X_SUBLANES]

    h = lax.fori_loop(0, ng, body, h_s[...], unroll=4)
    h_s[...] = h
    hf_ref[...] = h
    o_ref[...] = b_s[...].astype(BF16)


def _lru(u, rev, bsz, n, tt, cw, cb, wg, ba, bx, lam, h0):
    nt = n // tt
    main, prev, nxt = _scan_specs(tt, nt, bsz, rev, COL_LRU, BRANCH)

    def tix(i):
        return (nt - 1 - i) if rev else i

    full = lambda shape: pl.BlockSpec(shape, lambda b, i: (0,) * len(shape))
    return pl.pallas_call(
        functools.partial(_lru_kernel, rev),
        grid=(bsz, nt),
        in_specs=[main, prev, nxt, full((CONV_W, BRANCH)), full((1, BRANCH)),
                  full((BRANCH // V7X_LANES, V7X_LANES, 2 * V7X_LANES)),
                  full((1, BRANCH)), full((1, BRANCH)), full((1, BRANCH)),
                  pl.BlockSpec((None, 1, BRANCH), lambda b, i: (b, 0, 0))],
        out_specs=[pl.BlockSpec((tt, BRANCH), lambda b, i: (b * nt + tix(i), 0)),
                   pl.BlockSpec((None, 1, BRANCH), lambda b, i: (b, 0, 0))],
        out_shape=[jax.ShapeDtypeStruct((bsz * n, BRANCH), BF16),
                   jax.ShapeDtypeStruct((bsz, 1, BRANCH), F32)],
        scratch_shapes=[pltpu.VMEM((tt, BRANCH), F32), pltpu.VMEM((tt, BRANCH), F32),
                        pltpu.VMEM((1, BRANCH), F32)],
        compiler_params=_cparams(("arbitrary", "arbitrary")),
        name="lru_bwd" if rev else "lru_fwd",
    )(u, u, u, cw, cb, wg, ba, bx, lam, h0)


def _heads_l2norm(a):
    outs = []
    for h in range(GDN_HEADS):
        ah = a[:, h * GDN_DH:(h + 1) * GDN_DH]
        outs.append(ah * lax.rsqrt(jnp.sum(ah * ah, axis=-1, keepdims=True) + L2_EPS))
    return jnp.concatenate(outs, axis=1)


def _bdot(a, b):
    return jnp.einsum('ncs,nst->nct', a.astype(BF16), b.astype(BF16), preferred_element_type=F32)


def _gdn_prep_kernel(nt, x_ref, xp_ref, xn_ref, cw_ref, o_ref):
    ti = pl.program_id(0) % nt
    qkv = _dwconv(x_ref[...].astype(F32), _halo(xp_ref, ti == 0), _halo(xn_ref, ti == nt - 1), cw_ref[...])
    qkv = _silu(qkv)
    o_ref[:, :BRANCH] = (_heads_l2norm(qkv[:, :BRANCH]) * (GDN_DH ** -0.5)).astype(BF16)
    o_ref[:, BRANCH:2 * BRANCH] = _heads_l2norm(qkv[:, BRANCH:2 * BRANCH]).astype(BF16)
    o_ref[:, 2 * BRANCH:] = qkv[:, 2 * BRANCH:].astype(BF16)


def _gdn_prep(u, bsz, n, tt, cw):
    nt = n // tt
    r8 = tt // V7X_SUBLANES
    width = 3 * BRANCH
    cb = COL_GDN // width
    last8 = bsz * nt * r8 - 1
    return pl.pallas_call(
        functools.partial(_gdn_prep_kernel, nt),
        grid=(bsz * nt,),
        in_specs=[pl.BlockSpec((tt, width), lambda i: (i, cb)),
                  pl.BlockSpec((V7X_SUBLANES, width), lambda i: (jnp.maximum(i * r8 - 1, 0), cb)),
                  pl.BlockSpec((V7X_SUBLANES, width), lambda i: (jnp.minimum((i + 1) * r8, last8), cb)),
                  pl.BlockSpec((CONV_W, width), lambda i: (0, 0))],
        out_specs=pl.BlockSpec((tt, width), lambda i: (i, 0)),
        out_shape=jax.ShapeDtypeStruct((bsz * n, width), BF16),
        compiler_params=_cparams(("arbitrary",)),
        name="gdn_prep",
    )(u, u, u, cw)


def _gdn_kernel(rev, x_ref, ab_ref, alog_ref, dtb_ref, s0_ref,
                o_ref, sf_ref, kw_s, ku_s, qe_s, gl_s, o_s, s_s):
    i = pl.program_id(1)
    gbt, tt = x_ref.shape[0], x_ref.shape[1]
    rt = gbt * tt
    nc = tt // CHUNK
    nct = gbt * nc
    srows = GDN_HEADS * GDN_DH

    @pl.when(i == 0)
    def _():
        s_s[...] = s0_ref[...].reshape(gbt * srows, GDN_DH)

    x = x_ref[...].reshape(rt, 3 * BRANCH)
    q = x[:, :BRANCH].astype(F32)
    k = x[:, BRANCH:2 * BRANCH].astype(F32)
    v = x[:, 2 * BRANCH:].astype(F32)

    ab = ab_ref[...].reshape(rt, V7X_LANES)
    beta_c = _sigmoid(ab)
    gc = -jnp.exp(alog_ref[...]) * _softplus(ab + dtb_ref[...])
    rowc = lax.broadcasted_iota(jnp.int32, (rt, V7X_LANES), 0) & (CHUNK - 1)
    for s in (1 << b for b in range(LOG2_CHUNK)):
        if rev:
            sh, valid = pltpu.roll(gc, rt - s, 0), rowc < CHUNK - s
        else:
            sh, valid = pltpu.roll(gc, s, 0), rowc >= s
        gc = gc + jnp.where(valid, sh, 0.0)
    gcr_c = gc.reshape(nct, CHUNK, V7X_LANES)
    glr_c = gcr_c[:, 0:1, :] if rev else gcr_c[:, CHUNK - 1:CHUNK, :]
    ekd_c = jnp.exp(glr_c - gcr_c).reshape(rt, V7X_LANES)
    egl_c = jnp.broadcast_to(jnp.exp(glr_c), (nct, V7X_SUBLANES, V7X_LANES)).reshape(nct * V7X_SUBLANES, V7X_LANES)

    boff, aoff = _gdn_gate_cols(rev)

    def spread(a, off):
        return jnp.concatenate(
            [jnp.broadcast_to(a[:, off + h:off + h + 1], (a.shape[0], GDN_DH)) for h in range(GDN_HEADS)], axis=1)

    beta = spread(beta_c, boff)
    gcb = spread(gc, aoff)
    eg = spread(jnp.exp(gc), aoff)
    kd = k * spread(ekd_c, aoff)
    gl_s[...] = spread(egl_c, aoff).reshape(nct, V7X_SUBLANES, BRANCH)
    kb = k * beta
    qd = q * eg
    rhs_v = v * beta
    rhs_k = kb * eg

    ci = lax.broadcasted_iota(jnp.int32, (CHUNK, CHUNK), 0)
    si = lax.broadcasted_iota(jnp.int32, (CHUNK, CHUNK), 1)
    incl = (ci <= si) if rev else (ci >= si)
    strict = (ci < si) if rev else (ci > si)
    eye = (ci == si).astype(F32)
    same_blk = [(ci >> (3 + j)) == (si >> (3 + j)) for j in range(LOG2_CHUNK - 2)]

    nb = GDN_HEADS * nct

    def batched(a, col0=0):
        return jnp.concatenate(
            [a[:, col0 + h * GDN_DH:col0 + (h + 1) * GDN_DH].reshape(nct, CHUNK, GDN_DH) for h in range(GDN_HEADS)],
            axis=0)

    k_b = batched(x, BRANCH)
    q_b = batched(x)
    lhs = jnp.concatenate([batched(kb).astype(BF16), q_b], axis=1)
    kkqk = jnp.einsum('ncd,nsd->ncs', lhs, k_b, preferred_element_type=F32)
    gch = batched(gcb)
    gcol = gch[:, :, :CHUNK]
    grow = jnp.stack([gch[g].T[:CHUNK] for g in range(nb)], axis=0)
    diff = gcol - grow
    decay = jnp.where(incl, jnp.exp(jnp.where(incl, diff, 0.0)), 0.0)
    a_low = jnp.where(strict, kkqk[:, :CHUNK] * decay, 0.0)
    attn = jnp.where(incl, kkqk[:, CHUNK:] * decay, 0.0)
    d_blk = jnp.where(same_blk[0], a_low, 0.0)
    tinv = eye - d_blk
    m = d_blk
    for _ in range(2):
        m = _bdot(m, m)
        tinv = tinv + _bdot(tinv, m)
    for lvl in range(LOG2_CHUNK - 3):
        a_off = jnp.where(jnp.logical_and(same_blk[lvl + 1], jnp.logical_not(same_blk[lvl])), a_low, 0.0)
        tinv = tinv - _bdot(_bdot(tinv, a_off), tinv)
    rhs = jnp.concatenate([batched(rhs_v), batched(rhs_k)], axis=2)
    sol = rhs + _bdot(tinv - eye, rhs)
    kd_b = batched(kd)
    kdt = jnp.stack([kd_b[g].T for g in range(nb)], axis=0)
    m2 = _bdot(jnp.concatenate([kdt, attn], axis=1), sol)
    ku_s[...] = m2[:, :GDN_DH, :GDN_DH].reshape(GDN_HEADS, nct, GDN_DH, GDN_DH)
    kw_s[...] = m2[:, :GDN_DH, GDN_DH:].astype(BF16).reshape(GDN_HEADS, nct, GDN_DH, GDN_DH)
    qe = batched(qd) - m2[:, GDN_DH:, GDN_DH:]
    for h in range(GDN_HEADS):
        ls = slice(h * GDN_DH, (h + 1) * GDN_DH)
        o_s[:, ls] = m2[h * nct:(h + 1) * nct, GDN_DH:, :GDN_DH].reshape(rt, GDN_DH)
        qe_s[:, ls] = qe[h * nct:(h + 1) * nct].reshape(rt, GDN_DH).astype(BF16)

    def chunk_body(j, carry):
        for bb in range(gbt):
            cidx = bb * nc + ((nc - 1 - j) if rev else j)
            r0 = pl.multiple_of(cidx * CHUNK, CHUNK)
            for h in range(GDN_HEADS):
                ls = slice(h * GDN_DH, (h + 1) * GDN_DH)
                ss = slice(bb * srows + h * GDN_DH, bb * srows + (h + 1) * GDN_DH)
                s_h = s_s[ss, :]
                s_b = s_h.astype(BF16)
                o_s[pl.ds(r0, CHUNK), ls] += jnp.dot(qe_s[pl.ds(r0, CHUNK), ls], s_b, preferred_element_type=F32)
                gl = gl_s[cidx][0:1, ls]
                s_s[ss, :] = (s_h * gl - jnp.dot(kw_s[h, cidx], s_b, preferred_element_type=F32)) + ku_s[h, cidx]
        return carry

    lax.fori_loop(0, nc, chunk_body, 0)
    o_ref[...] = o_s[...].astype(BF16).reshape(gbt, tt, BRANCH)
    sf_ref[...] = s_s[...].reshape(gbt, srows, GDN_DH)


def _gdn(xn, ab, rev, bsz, n, tt, alog_c, dtb_c, s0):
    nt = n // tt
    gbt = 4 if bsz % 4 == 0 else (2 if bsz % 2 == 0 else 1)
    nct = gbt * (tt // CHUNK)
    rt = gbt * tt

    def tix(i):
        return (nt - 1 - i) if rev else i

    full = lambda shape: pl.BlockSpec(shape, lambda b, i: (0,) * len(shape))
    state = pl.BlockSpec((gbt, GDN_HEADS * GDN_DH, GDN_DH), lambda b, i: (b, 0, 0))
    o, sf = pl.pallas_call(
        functools.partial(_gdn_kernel, rev),
        grid=(bsz // gbt, nt),
        in_specs=[pl.BlockSpec((gbt, tt, 3 * BRANCH), lambda b, i: (b, tix(i), 0)),
                  pl.BlockSpec((gbt, tt, V7X_LANES), lambda b, i: (b, tix(i), 0)),
                  full((1, V7X_LANES)), full((1, V7X_LANES)), state],
        out_specs=[pl.BlockSpec((gbt, tt, BRANCH), lambda b, i: (b, tix(i), 0)), state],
        out_shape=[jax.ShapeDtypeStruct((bsz, n, BRANCH), BF16),
                   jax.ShapeDtypeStruct((bsz, GDN_HEADS * GDN_DH, GDN_DH), F32)],
        scratch_shapes=[pltpu.VMEM((GDN_HEADS, nct, GDN_DH, GDN_DH), BF16),
                        pltpu.VMEM((GDN_HEADS, nct, GDN_DH, GDN_DH), F32),
                        pltpu.VMEM((rt, BRANCH), BF16),
                        pltpu.VMEM((nct, V7X_SUBLANES, BRANCH), F32),
                        pltpu.VMEM((rt, BRANCH), F32),
                        pltpu.VMEM((gbt * GDN_HEADS * GDN_DH, GDN_DH), F32)],
        compiler_params=_cparams(("arbitrary", "arbitrary")),
        name="gdn_bwd" if rev else "gdn_fwd",
    )(xn.reshape(bsz, n, 3 * BRANCH), ab.reshape(bsz, n, V7X_LANES), alog_c, dtb_c, s0)
    return o.reshape(bsz * n, BRANCH), sf


def _rope(x, cos, sin_signed):
    lane = lax.broadcasted_iota(jnp.int32, x.shape, 1)
    first = (lane & 31) < 16
    w = x.shape[1]
    rot = jnp.where(first, pltpu.roll(x, w - 16, 1), pltpu.roll(x, 16, 1))
    return x * cos + rot * sin_signed


def _ret_kernel(rev, rope, *refs):
    if rope:
        (q_ref, k_ref, v_ref, cos_ref, sin_ref, dm_ref, qsc_ref, ksc_ref, gch_ref, s0_ref,
         o_ref, sf_ref, s_s) = refs
    else:
        (q_ref, k_ref, v_ref, dm_ref, qsc_ref, ksc_ref, gch_ref, s0_ref, o_ref, sf_ref, s_s) = refs
    i = pl.program_id(1)
    tt = q_ref.shape[0]
    nc = tt // RET_CHUNK

    @pl.when(i == 0)
    def _():
        s_s[...] = s0_ref[...]

    q = q_ref[...].astype(F32)
    k = k_ref[...].astype(F32)
    if rope:
        q = _rope(q, cos_ref[...], sin_ref[...])
        k = _rope(k, cos_ref[...], sin_ref[...])
    k = k * (RET_QK ** -0.5)
    q3 = q.reshape(nc, RET_CHUNK, RET_HEADS * RET_QK)
    k3 = k.reshape(nc, RET_CHUNK, RET_HEADS * RET_QK)
    qd3 = q3 * qsc_ref[...]
    kd3 = k3 * ksc_ref[...]
    v = v_ref[...]
    lane = lax.broadcasted_iota(jnp.int32, (nc, RET_CHUNK, V7X_LANES), 2)
    lo_half = lane < RET_QK

    def pair_lanes(a3, h, keep_other=False):
        pair = a3[:, :, (h // 2) * V7X_LANES:(h // 2 + 1) * V7X_LANES]
        if keep_other:
            return pair
        sel = lo_half if h % 2 == 0 else jnp.logical_not(lo_half)
        return jnp.where(sel, pair, 0.0)

    def batched(fn):
        return jnp.concatenate([fn(h) for h in range(RET_HEADS)], axis=0)

    nb = RET_HEADS * nc
    q_b = batched(lambda h: pair_lanes(q3, h)).astype(BF16)
    k_b = batched(lambda h: pair_lanes(k3, h, keep_other=True)).astype(BF16)
    v_b = batched(lambda h: v[:, h * RET_V:(h + 1) * RET_V].reshape(nc, RET_CHUNK, RET_V))
    dm_b = batched(lambda h: jnp.broadcast_to(dm_ref[h], (nc, RET_CHUNK, RET_CHUNK)))
    scores = jnp.einsum('ncd,nsd->ncs', q_b, k_b, preferred_element_type=F32) * dm_b
    o_intra = _bdot(scores, v_b)
    kd_b = batched(lambda h: pair_lanes(kd3, h))
    kdt = jnp.stack([kd_b[g].T for g in range(nb)], axis=0)
    kv = _bdot(kdt, v_b)
    qd_b = batched(lambda h: pair_lanes(qd3, h)).astype(BF16)

    order = range(nc - 1, -1, -1) if rev else range(nc)
    s_in = [None] * nb
    for h in range(RET_HEADS):
        gch = gch_ref[0:1, h * RET_V:(h + 1) * RET_V]
        s_h = s_s[h * V7X_LANES:(h + 1) * V7X_LANES, :]
        for n in order:
            s_in[h * nc + n] = s_h
            s_h = s_h * gch + kv[h * nc + n]
        s_s[h * V7X_LANES:(h + 1) * V7X_LANES, :] = s_h
    o = o_intra + _bdot(qd_b, jnp.stack(s_in, axis=0))
    for h in range(RET_HEADS):
        o_ref[:, h * RET_V:(h + 1) * RET_V] = o[h * nc:(h + 1) * nc].reshape(tt, RET_V).astype(BF16)
    sf_ref[...] = s_s[...]


def _ret(u, rev, bsz, n, tt, tabs, rope_tabs, s0):
    nt = n // tt
    dm, qsc, ksc, gch = tabs

    def tix(i):
        return (nt - 1 - i) if rev else i

    qk_w = RET_HEADS * RET_QK
    full = lambda shape: pl.BlockSpec(shape, lambda b, i: (0,) * len(shape))
    state = pl.BlockSpec((None, RET_HEADS * V7X_LANES, RET_V), lambda b, i: (b, 0, 0))
    in_specs = [pl.BlockSpec((tt, qk_w), lambda b, i: (b * nt + tix(i), COL_RETQ // qk_w)),
                pl.BlockSpec((tt, qk_w), lambda b, i: (b * nt + tix(i), COL_RETK // qk_w)),
                pl.BlockSpec((tt, BRANCH), lambda b, i: (b * nt + tix(i), COL_RETV // BRANCH))]
    args = [u, u, u]
    if rope_tabs is not None:
        in_specs += [pl.BlockSpec((tt, qk_w), lambda b, i: (tix(i), 0))] * 2
        args += list(rope_tabs)
    in_specs += [full(dm.shape), full(qsc.shape), full(ksc.shape), full(gch.shape), state]
    args += [dm, qsc, ksc, gch, s0]
    return pl.pallas_call(
        functools.partial(_ret_kernel, rev, rope_tabs is not None),
        grid=(bsz, nt),
        in_specs=in_specs,
        out_specs=[pl.BlockSpec((tt, BRANCH), lambda b, i: (b * nt + tix(i), 0)), state],
        out_shape=[jax.ShapeDtypeStruct((bsz * n, BRANCH), BF16),
                   jax.ShapeDtypeStruct((bsz, RET_HEADS * V7X_LANES, RET_V), F32)],
        scratch_shapes=[pltpu.VMEM((RET_HEADS * V7X_LANES, RET_V), F32)],
        compiler_params=_cparams(("arbitrary", "arbitrary")),
        name="ret_bwd" if rev else "ret_fwd",
    )(*args)


def _head_rms(a):
    outs = []
    for h in range(BRANCH // V7X_LANES):
        ah = a[:, h * V7X_LANES:(h + 1) * V7X_LANES]
        outs.append(ah * lax.rsqrt(jnp.mean(ah * ah, axis=-1, keepdims=True) + NORM_EPS))
    return jnp.concatenate(outs, axis=1)


def _merge_kernel(final, ona_ref, lf_ref, lb_ref, gf_ref, gb_ref, rf_ref, rb_ref, z_ref, mg_ref, x_ref,
                  gate_ref, gng_ref, wb_ref, wo_ref, fg_ref, o_ref):
    sz = _silu(z_ref[...].astype(F32))
    ya = ona_ref[...].astype(F32) * sz[:, :BRANCH]
    yb = (lf_ref[...].astype(F32) + lb_ref[...].astype(F32)) * sz[:, BRANCH:2 * BRANCH]
    oc = gf_ref[...].astype(F32) + gb_ref[...].astype(F32)
    yc = _head_rms(oc) * gng_ref[...] * sz[:, 2 * BRANCH:3 * BRANCH]
    od = rf_ref[...].astype(F32) + rb_ref[...].astype(F32)
    yd = _head_rms(od) * sz[:, 3 * BRANCH:]
    acc = None
    for idx, y in enumerate((ya, yb, yc, yd)):
        proj = jnp.dot(y.astype(BF16), wb_ref[idx], preferred_element_type=F32)
        gate = _sigmoid(mg_ref[:, idx * D_MODEL:(idx + 1) * D_MODEL].astype(F32))
        acc = gate * proj if acc is None else acc + gate * proj
    out = jnp.dot(acc.astype(BF16), wo_ref[...], preferred_element_type=F32)
    xn = x_ref[...] + gate_ref[...] * out
    if final:
        ms = jnp.mean(xn * xn, axis=-1, keepdims=True)
        xn = xn * lax.rsqrt(ms + NORM_EPS) * fg_ref[...]
    o_ref[...] = xn


def _merge(final, o_na, lru, gdn, ret, u, x2, gate, gng, wb, wo, fg, tm, tiles_per_mod):
    t = x2.shape[0]
    br = pl.BlockSpec((tm, BRANCH), lambda i: (i, 0))
    full = lambda shape: pl.BlockSpec(shape, lambda i: (0,) * len(shape))
    return pl.pallas_call(
        functools.partial(_merge_kernel, final),
        grid=(t // tm,),
        in_specs=[br] * 7 + [
            pl.BlockSpec((tm, 4 * BRANCH), lambda i: (i, COL_Z // (4 * BRANCH))),
            pl.BlockSpec((tm, N_BRANCH * D_MODEL), lambda i: (i, 0)),
            pl.BlockSpec((tm, D_MODEL), lambda i: (i, 0)),
            pl.BlockSpec((None, 1, D_MODEL), lambda i: (i // tiles_per_mod, 0, 0)),
            full((1, BRANCH)), full((N_BRANCH, BRANCH, D_MODEL)), full((D_MODEL, D_MODEL)),
            full((1, D_MODEL))],
        out_specs=pl.BlockSpec((tm, D_MODEL), lambda i: (i, 0)),
        out_shape=jax.ShapeDtypeStruct((t, D_MODEL), F32),
        compiler_params=_cparams(("arbitrary",)),
        name="merge",
    )(o_na, lru[0], lru[1], gdn[0], gdn[1], ret[0], ret[1], u, u, x2, gate, gng, wb, wo, fg)


def _permute_w_in(w):
    def cols(name):
        o, n = _SRC[name]
        return w[:, o:o + n]
    w_p = jnp.concatenate([cols('merge'), cols('na_z'), cols('lru_z'), cols('gdn_z'), cols('ret_z'),
                           cols('gdn_qkv'), cols('na_q'), cols('na_k'), cols('na_v'), cols('lru_x'),
                           cols('ret_q'), cols('ret_k'), cols('ret_v')], axis=1)
    w_ab = jnp.pad(cols('gdn_ab'), ((0, 0), (0, V7X_LANES - 4 * GDN_HEADS)))
    return w_p.astype(BF16), w_ab.astype(BF16)


def _na_bias_tables(rpb):
    col = np.arange(GRID_W)
    col_start = np.clip(col - NA_WIN_COLS // 2, 0, GRID_W - NA_WIN_COLS)
    in_win = (col[None, :] >= col_start[:, None]) & (col[None, :] < col_start[:, None] + NA_WIN_COLS)
    dc = np.clip(col[None, :] - col[:, None] + NA_WIN_COLS - 1, 0, 2 * NA_WIN_COLS - 2)
    onehot = (np.arange(2 * NA_WIN_COLS - 1)[:, None, None] == dc[None]).astype(np.float32)
    t = jnp.einsum('hdj,jqk->hdqk', rpb.astype(F32), onehot, precision=lax.Precision.HIGHEST)
    t = jnp.where(in_win[None, None], t, NEG_INF)
    tv = jnp.stack([t[:, ds:ds + NA_WIN_ROWS] for ds in range(NA_WIN_ROWS)], axis=0)
    return jnp.transpose(tv, (0, 1, 3, 2, 4)).reshape(NA_WIN_ROWS, NA_HEADS, GRID_W, NA_WIN_ROWS * GRID_W)


def _lru_gate_weights(wa, wx):
    def chunk(w, c):
        z = jnp.zeros((LRU_BLOCK, LRU_BLOCK), w.dtype)
        return jnp.concatenate([jnp.concatenate([w[2 * c], z], axis=1),
                                jnp.concatenate([z, w[2 * c + 1]], axis=1)], axis=0)
    return jnp.stack([jnp.concatenate([chunk(wa, c), chunk(wx, c)], axis=1)
                      for c in range(LRU_BLOCKS // 2)], axis=0).astype(BF16)


def _rope_tables(n):
    t = np.arange(n)
    row = (t // GRID_W).astype(np.float32)
    col = (t % GRID_W).astype(np.float32)
    quarter = RET_QK // 4
    inv = jnp.asarray(ROPE_BASE, F32) ** (-jnp.arange(quarter, dtype=F32) / quarter)
    ang_r = jnp.asarray(row)[:, None] * inv[None]
    ang_c = jnp.asarray(col)[:, None] * inv[None]
    ang = jnp.concatenate([ang_r, ang_r, ang_c, ang_c], axis=-1)
    cos, sin = jnp.cos(ang), jnp.sin(ang)
    sign = np.where((np.arange(RET_QK) % 32) < 16, -1.0, 1.0).astype(np.float32)
    return jnp.tile(cos, (1, RET_HEADS)), jnp.tile(sin * sign[None], (1, RET_HEADS))


def _ret_tables(rev):
    log_gamma = jnp.log1p(-jnp.exp2(-(5.0 + jnp.arange(RET_HEADS, dtype=F32))))
    pos = jnp.arange(RET_CHUNK, dtype=F32)
    if rev:
        pos = pos[::-1]
    rel = pos[:, None] - pos[None, :]
    dm = jnp.where(rel >= 0, jnp.exp(jnp.maximum(rel, 0.0)[None] * log_gamma[:, None, None]), 0.0)
    ksc = jnp.exp((RET_CHUNK - 1 - pos)[None] * log_gamma[:, None])
    qsc = jnp.exp((pos + 1.0)[None] * log_gamma[:, None])
    widen = lambda a: jnp.repeat(a.T, RET_QK, axis=1)
    gch = jnp.broadcast_to(jnp.repeat(jnp.exp(RET_CHUNK * log_gamma), RET_V)[None],
                           (V7X_SUBLANES, RET_HEADS * RET_V))
    return dm, widen(qsc), widen(ksc), gch


def _gdn_gate_cols(rev):
    return (GDN_HEADS if rev else 0), (3 * GDN_HEADS if rev else 2 * GDN_HEADS)


def _gdn_gate_row(vals, rev):
    _, aoff = _gdn_gate_cols(rev)
    return jnp.zeros((1, V7X_LANES), F32).at[0, aoff:aoff + GDN_HEADS].set(vals.astype(F32))


def _layer(x2, c2, bsz, n, n_ctx, mod, p, consts, with_ctx_out, final):
    shift_x = mod[:bsz, None, :D_MODEL]
    scale_x = mod[:bsz, None, D_MODEL:2 * D_MODEL]
    gate_x = mod[:bsz, None, 2 * D_MODEL:]
    shift_c = mod[bsz:bsz + 1, None, :D_MODEL]
    scale_c = mod[bsz:bsz + 1, None, D_MODEL:2 * D_MODEL]
    gate_c = mod[bsz:bsz + 1, None, 2 * D_MODEL:]

    tm = 1024 if n % 1024 == 0 else 512
    tmc = min(1024, bsz * n_ctx)
    tt = 256
    tt_ret = 1024 if n % 1024 == 0 else 512
    big = 1 << 30
    u, ab = _inproj(x2, p['norm_g'], shift_x, scale_x, p['w_p'], p['w_ab'], tm, n // tm)
    uc, abc = _inproj(c2, p['norm_g'], shift_c, scale_c, p['w_p'], p['w_ab'], tmc, big)

    o_na = _na_latent(u, uc, p['na_grp'], bsz, n, n_ctx)
    gdn_x = _gdn_prep(u, bsz, n, tt, p['gdn_cw'])
    gdn_xc = _gdn_prep(uc, bsz, n_ctx, n_ctx, p['gdn_cw'])

    lru_o, lru_oc = [], []
    gdn_o, gdn_oc = [], []
    ret_o, ret_oc = [], []
    for d, rev in enumerate((False, True)):
        h0 = jnp.zeros((bsz, 1, BRANCH), F32)
        lru_args = (p['lru_cw'], p['lru_cb'], p['lru_wg'][d], p['lru_ba'][d], p['lru_bx'][d], p['lru_lam'][d])
        oc_, hc = _lru(uc, rev, bsz, n_ctx, n_ctx, *lru_args, h0)
        ol_, _ = _lru(u, rev, bsz, n, tt_ret, *lru_args, hc)
        lru_o.append(ol_)
        lru_oc.append(oc_)

        s0 = jnp.zeros((bsz, GDN_HEADS * GDN_DH, GDN_DH), F32)
        gdn_args = (p['gdn_alog'][d], p['gdn_dtb'][d])
        oc_, sc = _gdn(gdn_xc, abc, rev, bsz, n_ctx, n_ctx, *gdn_args, s0)
        ol_, _ = _gdn(gdn_x, ab, rev, bsz, n, tt, *gdn_args, sc)
        gdn_o.append(ol_)
        gdn_oc.append(oc_)

        r0 = jnp.zeros((bsz, RET_HEADS * V7X_LANES, RET_V), F32)
        oc_, rc = _ret(uc, rev, bsz, n_ctx, n_ctx, consts['ret_tabs'][d], None, r0)
        ol_, _ = _ret(u, rev, bsz, n, tt_ret, consts['ret_tabs'][d], consts['rope'], rc)
        ret_o.append(ol_)
        ret_oc.append(oc_)

    merge_w = (p['gdn_ng'], p['w_branch'], p['w_out'], consts['final_g'])
    x_new = _merge(final, o_na, lru_o, gdn_o, ret_o, u, x2, gate_x, *merge_w, 512, n // 512)
    c_new = None
    if with_ctx_out:
        o_nac = _na_ctx(uc, bsz, n_ctx)
        c_new = _merge(False, o_nac, lru_oc, gdn_oc, ret_oc, uc, c2, gate_c, *merge_w, n_ctx, big)
    return x_new, c_new


def kernel(x, c, ctx, c_ctx, norm_g, w_mod, b_mod, w_in, na_rpb, lru_conv_w, lru_conv_b, lru_wa, lru_ba,
           lru_wx, lru_bx, lru_lam, gdn_conv_w, gdn_a_log, gdn_dt_bias, gdn_norm_g, w_branch, w_out,
           final_norm_g):
    bsz, n, d = x.shape
    n_ctx = ctx.shape[1]
    depth = w_in.shape[0]
    assert d == D_MODEL and w_in.shape[2] == D_IN
    assert n % 512 == 0 and n_ctx % max(CHUNK, RET_CHUNK) == 0 and (bsz * n_ctx) % 8 == 0

    rows = -(-(bsz + 1) // V7X_SUBLANES) * V7X_SUBLANES
    cc = jnp.zeros((rows, D_MODEL), F32).at[:bsz].set(c).at[bsz].set(c_ctx)
    mods = _modulation(cc, w_mod, b_mod)

    consts = {
        'rope': _rope_tables(n),
        'ret_tabs': (_ret_tables(False), _ret_tables(True)),
        'final_g': final_norm_g.reshape(1, D_MODEL),
    }

    x2 = x.reshape(bsz * n, D_MODEL)
    c2 = ctx.reshape(bsz * n_ctx, D_MODEL)
    for layer in range(depth):
        w_p, w_ab = _permute_w_in(w_in[layer])
        p = {
            'na_grp': _na_bias_tables(na_rpb[layer]),
            'norm_g': norm_g[layer].reshape(1, D_MODEL),
            'w_p': w_p, 'w_ab': w_ab,
            'lru_cw': lru_conv_w[layer], 'lru_cb': lru_conv_b[layer].reshape(1, BRANCH),
            'lru_wg': [_lru_gate_weights(lru_wa[layer, dd], lru_wx[layer, dd]) for dd in range(2)],
            'lru_ba': [lru_ba[layer, dd].reshape(1, BRANCH) for dd in range(2)],
            'lru_bx': [lru_bx[layer, dd].reshape(1, BRANCH) for dd in range(2)],
            'lru_lam': [lru_lam[layer, dd].reshape(1, BRANCH) for dd in range(2)],
            'gdn_cw': gdn_conv_w[layer],
            'gdn_alog': [_gdn_gate_row(gdn_a_log[layer, dd], dd == 1) for dd in range(2)],
            'gdn_dtb': [_gdn_gate_row(gdn_dt_bias[layer, dd], dd == 1) for dd in range(2)],
            'gdn_ng': jnp.tile(gdn_norm_g[layer].astype(F32), GDN_HEADS)[None],
            'w_branch': w_branch[layer].astype(BF16),
            'w_out': w_out[layer].astype(BF16),
        }
        last = layer == depth - 1
        x2, c2 = _layer(x2, c2, bsz, n, n_ctx, mods[layer], p, consts, not last, last)
    return x2.reshape(bsz, n, D_MODEL)
```
